```python
import math
import jax, jax.numpy as jnp
from jax import lax
import numpy as np

D_MODEL = 1024
BATCH = 4
SEQ = 8192
DEPTH = 2
DEC_BATCH = 2
DEC_SEQ = 8192
PAST_LEN = 128

DN_HEADS = 4
DN_HEAD_DIM = 64
DN_WIDTH = DN_HEADS * DN_HEAD_DIM
DN_CONV = 5
DN_CHUNK = 64
DIFF_HEADS = 4
DIFF_HEAD_DIM = 64
DIFF_WIDTH = DIFF_HEADS * 2 * DIFF_HEAD_DIM
SWA_HEADS = 4
SWA_KV_HEADS = 2
SWA_HEAD_DIM = 64
SWA_WIDTH = SWA_HEADS * SWA_HEAD_DIM
WINDOW = 128
BLOCK = 128
ROPE_THETA = 10000.0
MIX_WIDTH = DN_WIDTH + DIFF_WIDTH + SWA_WIDTH
D_FF = 4 * D_MODEL
EPS = 1e-6
IN_SIZES = (3 * DN_WIDTH,
            DN_WIDTH,
            2 * DN_HEADS,
            2 * DN_HEADS,
            2 * DIFF_HEADS * DIFF_HEAD_DIM,
            2 * DIFF_HEADS * DIFF_HEAD_DIM,
            DIFF_WIDTH,
            SWA_HEADS * SWA_HEAD_DIM,
            SWA_KV_HEADS * SWA_HEAD_DIM,
            SWA_KV_HEADS * SWA_HEAD_DIM)
IN_COLS = sum(IN_SIZES)

kernel_name = "hybrid_bidir_deltanet_diffattn_swa_encoder"


def _rmsnorm(x, w):
    xf = x.astype(jnp.float32)
    y = xf * lax.rsqrt(jnp.mean(xf * xf, axis=-1, keepdims=True) + EPS)
    return (y * w.astype(jnp.float32)).astype(x.dtype)


def _l2norm(x):
    xf = x.astype(jnp.float32)
    return xf * lax.rsqrt(jnp.sum(xf * xf, axis=-1, keepdims=True) + EPS)


def _rope(x, pos):
    half = x.shape[-1] // 2
    inv = ROPE_THETA ** (-jnp.arange(half, dtype=jnp.float32) / half)
    ang = pos[:, None] * inv[None, :]
    shape = (1, x.shape[1]) + (1,) * (x.ndim - 3) + (half,)
    cos = jnp.cos(ang).reshape(shape)
    sin = jnp.sin(ang).reshape(shape)
    xf = x.astype(jnp.float32)
    x1, x2 = xf[..., :half], xf[..., half:]
    return jnp.concatenate([x1 * cos - x2 * sin, x2 * cos + x1 * sin], axis=-1).astype(x.dtype)


def _short_conv(x, w):
    k, c = w.shape
    return lax.conv_general_dilated(
        x, w[:, None, :].astype(x.dtype), window_strides=(1,),
        padding=[(k // 2, k // 2)], dimension_numbers=("NWC", "WIO", "NWC"),
        feature_group_count=c)


def _gated_delta_rule(q, k, v, g, beta):
    f32 = jnp.float32
    b, s, h, dk = q.shape
    dv = v.shape[-1]
    c = DN_CHUNK
    n = s // c

    def chunks(t):
        t = t.astype(f32).reshape((b, n, c, h) + t.shape[3:])
        return jnp.moveaxis(t, 3, 1)

    q = chunks(q) * (dk ** -0.5)
    k = chunks(k)
    v = chunks(v)
    beta = chunks(beta)
    g = jnp.cumsum(chunks(g), axis=-1)
    tril = jnp.tril(jnp.ones((c, c), dtype=bool))
    strict = jnp.tril(jnp.ones((c, c), dtype=bool), -1)
    gdiff = g[..., :, None] - g[..., None, :]
    decay = jnp.where(tril, jnp.exp(jnp.where(tril, gdiff, 0.0)), 0.0)
    k_beta = k * beta[..., None]
    lmat = jnp.where(strict, jnp.einsum("bhncd,bhnmd->bhncm", k_beta, k) * decay, 0.0)
    amat = lmat + jnp.eye(c, dtype=f32)
    rhs = jnp.concatenate([v * beta[..., None], k_beta * jnp.exp(g)[..., None]], axis=-1)
    sol = lax.linalg.triangular_solve(amat, rhs, left_side=True, lower=True,
                                      unit_diagonal=True)
    u, w = sol[..., :dv], sol[..., dv:]
    a_intra = jnp.where(tril, jnp.einsum("bhncd,bhnmd->bhncm", q, k) * decay, 0.0)
    g_last = g[..., -1]
    q_dec = q * jnp.exp(g)[..., None]
    k_dec = k * jnp.exp(g_last[..., None] - g)[..., None]
    xs = (jnp.moveaxis(q_dec, 2, 0), jnp.moveaxis(k_dec, 2, 0), jnp.moveaxis(u, 2, 0),
          jnp.moveaxis(w, 2, 0), jnp.moveaxis(a_intra, 2, 0), jnp.moveaxis(g_last, 2, 0))

    def step(state, inp):
        qd, kd, ui, wi, ai, gl = inp
        v_new = ui - jnp.einsum("bhcd,bhde->bhce", wi, state)
        out = (jnp.einsum("bhcd,bhde->bhce", qd, state)
               + jnp.einsum("bhcm,bhme->bhce", ai, v_new))
        state = state * jnp.exp(gl)[..., None, None] + jnp.einsum("bhcd,bhce->bhde", kd, v_new)
        return state, out

    s0 = jnp.zeros((b, h, dk, dv), f32)
    _, o = lax.scan(step, s0, xs)
    return jnp.transpose(o, (1, 0, 3, 2, 4)).reshape(b, s, h, dv)


def _diff_attention(q, k, v, lam):
    b, s, h, _, d = q.shape
    nb = s // BLOCK
    q = q * (d ** -0.5)
    qb = jnp.moveaxis(q.reshape(b, nb, BLOCK, h, 2, d), 1, 0)

    def one_block(qi):
        sc = jnp.einsum("bqhjd,bkhjd->bhjqk", qi, k, preferred_element_type=jnp.float32)
        p = jax.nn.softmax(sc, axis=-1)
        a = p[:, :, 0] - lam * p[:, :, 1]
        return jnp.einsum("bhqk,bkhe->bqhe", a.astype(v.dtype), v)

    o = lax.map(one_block, qb)
    return jnp.moveaxis(o, 0, 1).reshape(b, s, h, 2 * d)


def _window_attention(q, k, v, sink):
    b, s, h, d = q.shape
    kv = k.shape[2]
    grp = h // kv
    nb = s // BLOCK
    qb = (q * (d ** -0.5)).reshape(b, nb, BLOCK, kv, grp, d)
    pad = ((0, 0), (BLOCK, BLOCK), (0, 0), (0, 0))
    kp = jnp.pad(k, pad)
    vp = jnp.pad(v, pad)

    def band(t):
        return jnp.concatenate(
            [t[:, i * BLOCK:i * BLOCK + s].reshape(b, nb, BLOCK, kv, d) for i in range(3)], axis=2)

    kb, vb = band(kp), band(vp)
    sc = jnp.einsum("bnqhgd,bnkhd->bnhgqk", qb, kb, preferred_element_type=jnp.float32)
    blk = jnp.arange(nb)[:, None, None] * BLOCK
    qpos = blk + jnp.arange(BLOCK)[None, :, None]
    kpos = blk - BLOCK + jnp.arange(3 * BLOCK)[None, None, :]
    mask = (jnp.abs(kpos - qpos) <= WINDOW) & (kpos >= 0) & (kpos < s)
    sc = jnp.where(mask[None, :, None, None], sc, -jnp.inf)
    sink_col = jnp.broadcast_to(sink.astype(jnp.float32).reshape(1, 1, kv, grp, 1, 1),
                                sc.shape[:-1] + (1,))
    p = jax.nn.softmax(jnp.concatenate([sc, sink_col], axis=-1), axis=-1)[..., :-1]
    o = jnp.einsum("bnhgqk,bnkhd->bnqhgd", p.astype(v.dtype), vb)
    return o.reshape(b, s, h, d)


def _layer(x, li, norm1_w, w_in, dn_conv_w, dn_a_log, dn_dt_bias, dn_norm_w,
           diff_lambda, diff_norm_w, swa_sink, w_out, norm2_w, w_up, w_down):
    b, s, _ = x.shape
    pos = jnp.arange(s, dtype=jnp.float32)
    hn = _rmsnorm(x, norm1_w)
    proj = hn @ w_in
    offs = []
    acc = 0
    for sz in IN_SIZES[:-1]:
        acc += sz
        offs.append(acc)
    (dn_qkv, dn_z, dn_b, dn_a, df_q, df_k, df_v, sw_q, sw_k, sw_v) = jnp.split(proj, offs, axis=-1)

    qkv = jax.nn.silu(_short_conv(dn_qkv, dn_conv_w))
    q, k, v = jnp.split(qkv, 3, axis=-1)
    q = _l2norm(q.reshape(b, s, DN_HEADS, DN_HEAD_DIM))
    k = _l2norm(k.reshape(b, s, DN_HEADS, DN_HEAD_DIM))
    v = v.reshape(b, s, DN_HEADS, DN_HEAD_DIM)
    beta = jax.nn.sigmoid(dn_b.astype(jnp.float32)).reshape(b, s, 2, DN_HEADS)
    g = (-jnp.exp(dn_a_log.astype(jnp.float32))
         * jax.nn.softplus(dn_a.astype(jnp.float32).reshape(b, s, 2, DN_HEADS)
                           + dn_dt_bias.astype(jnp.float32)))
    o_fwd = _gated_delta_rule(q, k, v, g[:, :, 0], beta[:, :, 0])
    flip = lambda t: jnp.flip(t, axis=1)
    o_bwd = flip(_gated_delta_rule(flip(q), flip(k), flip(v), flip(g[:, :, 1]), flip(beta[:, :, 1])))
    z = dn_z.astype(jnp.float32).reshape(b, s, DN_HEADS, DN_HEAD_DIM)
    dn_out = (_rmsnorm(o_fwd + o_bwd, dn_norm_w) * jax.nn.silu(z)).reshape(b, s, DN_WIDTH)

    dq = _rope(df_q.reshape(b, s, DIFF_HEADS, 2, DIFF_HEAD_DIM), pos)
    dk = _rope(df_k.reshape(b, s, DIFF_HEADS, 2, DIFF_HEAD_DIM), pos)
    dv = df_v.reshape(b, s, DIFF_HEADS, 2 * DIFF_HEAD_DIM)
    lam_init = 0.8 - 0.6 * math.exp(-0.3 * li)
    lamv = diff_lambda.astype(jnp.float32)
    lam = (jnp.exp(jnp.sum(lamv[0] * lamv[1])) - jnp.exp(jnp.sum(lamv[2] * lamv[3]))
           + lam_init)
    diff_o = _diff_attention(dq, dk, dv, lam)
    diff_out = (_rmsnorm(diff_o, diff_norm_w) * (1.0 - lam_init)).reshape(b, s, DIFF_WIDTH)

    sq = _rope(sw_q.reshape(b, s, SWA_HEADS, SWA_HEAD_DIM), pos)
    sk = _rope(sw_k.reshape(b, s, SWA_KV_HEADS, SWA_HEAD_DIM), pos)
    sv = sw_v.reshape(b, s, SWA_KV_HEADS, SWA_HEAD_DIM)
    swa_out = _window_attention(sq, sk, sv, swa_sink).reshape(b, s, SWA_WIDTH)

    mix = jnp.concatenate([dn_out.astype(x.dtype), diff_out.astype(x.dtype),
                           swa_out.astype(x.dtype)], axis=-1)
    x = x + mix @ w_out

    hn2 = _rmsnorm(x, norm2_w)
    x = x + jnp.square(jax.nn.relu(hn2 @ w_up)) @ w_down
    return x


def _trunk(x, norm1_w, w_in, dn_conv_w, dn_a_log, dn_dt_bias, dn_norm_w, diff_lambda,
           diff_norm_w, swa_sink, w_out, norm2_w, w_up, w_down, final_norm_w):
    for li in range(DEPTH):
        x = _layer(x, li, norm1_w[li], w_in[li], dn_conv_w[li], dn_a_log[li], dn_dt_bias[li],
                   dn_norm_w[li], diff_lambda[li], diff_norm_w[li], swa_sink[li], w_out[li],
                   norm2_w[li], w_up[li], w_down[li])
    return _rmsnorm(x, final_norm_w)


def setup_inputs(seed: int = 0) -> dict:
    key = jax.random.key(seed)
    ks = jax.random.split(key, 17)
    f32 = jnp.float32

    def nrm(k, shape, scale):
        return scale * jax.random.normal(k, shape, f32)

    x_prompt = nrm(ks[0], (BATCH, SEQ, D_MODEL), 1.0)
    x_sample = nrm(ks[1], (DEC_BATCH, DEC_SEQ, D_MODEL), 1.0)
    norm1_w = 1.0 + nrm(ks[2], (DEPTH, D_MODEL), 0.02)
    w_in = nrm(ks[3], (DEPTH, D_MODEL, IN_COLS), D_MODEL ** -0.5)
    dn_conv_w = nrm(ks[4], (DEPTH, DN_CONV, 3 * DN_WIDTH), DN_CONV ** -0.5)
    dn_a_log = jnp.log(jax.random.uniform(ks[5], (DEPTH, 2, DN_HEADS), f32, 1.0, 16.0))
    dt = jnp.exp(jax.random.uniform(ks[6], (DEPTH, 2, DN_HEADS), f32,
                                    math.log(1e-3), math.log(1e-1)))
    dn_dt_bias = dt + jnp.log(-jnp.expm1(-dt))
    dn_norm_w = 1.0 + nrm(ks[7], (DEPTH, DN_HEAD_DIM), 0.02)
    diff_lambda = nrm(ks[8], (DEPTH, 4, DIFF_HEAD_DIM), 0.1)
    diff_norm_w = 1.0 + nrm(ks[9], (DEPTH, 2 * DIFF_HEAD_DIM), 0.02)
    swa_sink = nrm(ks[10], (DEPTH, SWA_HEADS), 0.5)
    w_out = nrm(ks[11], (DEPTH, MIX_WIDTH, D_MODEL), MIX_WIDTH ** -0.5)
    norm2_w = 1.0 + nrm(ks[12], (DEPTH, D_MODEL), 0.02)
    w_up = nrm(ks[13], (DEPTH, D_MODEL, D_FF), D_MODEL ** -0.5)
    w_down = nrm(ks[14], (DEPTH, D_FF, D_MODEL), D_FF ** -0.5)
    final_norm_w = 1.0 + nrm(ks[15], (D_MODEL,), 0.02)
    return {"x_prompt": x_prompt, "x_sample": x_sample, "norm1_w": norm1_w, "w_in": w_in,
            "dn_conv_w": dn_conv_w, "dn_a_log": dn_a_log, "dn_dt_bias": dn_dt_bias,
            "dn_norm_w": dn_norm_w, "diff_lambda": diff_lambda, "diff_norm_w": diff_norm_w,
            "swa_sink": swa_sink, "w_out": w_out, "norm2_w": norm2_w, "w_up": w_up,
            "w_down": w_down, "final_norm_w": final_norm_w}


def reference(x_prompt, x_sample, norm1_w, w_in, dn_conv_w, dn_a_log, dn_dt_bias, dn_norm_w,
              diff_lambda, diff_norm_w, swa_sink, w_out, norm2_w, w_up, w_down, final_norm_w):
    y_prompt = _trunk(x_prompt, norm1_w, w_in, dn_conv_w, dn_a_log, dn_dt_bias, dn_norm_w,
                      diff_lambda, diff_norm_w, swa_sink, w_out, norm2_w, w_up, w_down,
                      final_norm_w)
    y_sample = _trunk(x_sample, norm1_w, w_in, dn_conv_w, dn_a_log, dn_dt_bias, dn_norm_w,
                      diff_lambda, diff_norm_w, swa_sink, w_out, norm2_w, w_up, w_down,
                      final_norm_w)
    return (y_prompt, y_sample)
```

```python
import functools
import math

import numpy as np
import jax
import jax.numpy as jnp
from jax import lax
from jax.experimental import pallas as pl
from jax.experimental.pallas import tpu as pltpu

F32 = jnp.float32
BF16 = jnp.bfloat16

D_MODEL = 1024
DN_HEADS = 4
DN_HEAD_DIM = 64
DN_WIDTH = DN_HEADS * DN_HEAD_DIM
DN_CONV = 5
DN_CHUNK = 64
DIFF_HEADS = 4
DIFF_HEAD_DIM = 64
DIFF_WIDTH = DIFF_HEADS * 2 * DIFF_HEAD_DIM
SWA_HEADS = 4
SWA_KV_HEADS = 2
SWA_HEAD_DIM = 64
SWA_WIDTH = SWA_HEADS * SWA_HEAD_DIM
WINDOW = 128
ROPE_THETA = 10000.0
D_FF = 4 * D_MODEL
EPS = 1e-6

LANES = 128
HALO = 16
VMEM_LIMIT = 56 * 1024 * 1024

_SEC_SIZES = (3 * DN_WIDTH, DN_WIDTH, 2 * DN_WIDTH, 2 * DN_WIDTH,
              DIFF_WIDTH, DIFF_WIDTH, DIFF_WIDTH, SWA_WIDTH, SWA_WIDTH, SWA_WIDTH)
_SEC_OFFS = tuple(int(v) for v in np.cumsum((0,) + _SEC_SIZES))
IN_COLS_R = _SEC_OFFS[-1]


def _in_col_index():
    o_z = 3 * DN_WIDTH
    o_b = o_z + DN_WIDTH
    o_a = o_b + 2 * DN_HEADS
    o_dq = o_a + 2 * DN_HEADS
    o_dk = o_dq + DIFF_WIDTH
    o_dv = o_dk + DIFF_WIDTH
    o_sq = o_dv + DIFF_WIDTH
    o_sk = o_sq + SWA_WIDTH
    o_sv = o_sk + SWA_KV_HEADS * SWA_HEAD_DIM
    grp = SWA_HEADS // SWA_KV_HEADS
    kv_rep = np.concatenate([np.tile(np.arange(SWA_HEAD_DIM) + g * SWA_HEAD_DIM, grp)
                             for g in range(SWA_KV_HEADS)])
    idx = np.concatenate([
        np.arange(0, o_z), np.arange(o_z, o_b),
        np.repeat(np.arange(o_b, o_a), DN_HEAD_DIM),
        np.repeat(np.arange(o_a, o_dq), DN_HEAD_DIM),
        np.arange(o_dq, o_dk), np.arange(o_dk, o_dv), np.arange(o_dv, o_sq),
        np.arange(o_sq, o_sk), o_sk + kv_rep, o_sv + kv_rep])
    assert idx.shape[0] == IN_COLS_R
    return idx.astype(np.int32)


def _rope_tables(seq):
    half = DIFF_HEAD_DIM // 2
    inv = ROPE_THETA ** (-jnp.arange(half, dtype=F32) / half)
    ang = jnp.arange(seq, dtype=F32)[:, None] * inv[None, :]
    cos, sin = jnp.cos(ang), jnp.sin(ang)
    reps = LANES // (2 * half)
    cos_t = jnp.tile(jnp.concatenate([cos, cos], axis=1), (1, reps))
    sin_t = jnp.tile(jnp.concatenate([-sin, sin], axis=1), (1, reps))
    return cos_t, sin_t


def _sigmoid(x):
    return 1.0 / (1.0 + jnp.exp(-x))


def _silu(x):
    return x * _sigmoid(x)


def _inproj_kernel(x_ref, nw_ref, w_ref, cos_ref, sin_ref, alog_ref, dtb_ref,
                   qkv_ref, z_ref, beta_ref, g_ref, dq_ref, dk_ref, dv_ref,
                   sq_ref, sk_ref, sv_ref):
    x = x_ref[...]
    ms = jnp.mean(x * x, axis=-1, keepdims=True)
    hn = (x * lax.rsqrt(ms + EPS) * nw_ref[...]).astype(BF16)
    cos = cos_ref[...]
    sin = sin_ref[...]
    lane = lax.broadcasted_iota(jnp.int32, cos.shape, 1)
    low = (lane % DIFF_HEAD_DIM) < (DIFF_HEAD_DIM // 2)

    def proj(sec):
        return jnp.dot(hn, w_ref[:, _SEC_OFFS[sec]:_SEC_OFFS[sec + 1]],
                       preferred_element_type=F32)

    def rope_store(sec, out_ref, scale):
        y_all = proj(sec)
        for c0 in range(0, _SEC_SIZES[sec], LANES):
            y = y_all[:, c0:c0 + LANES]
            up = pltpu.roll(y, DIFF_HEAD_DIM // 2, axis=1)
            down = pltpu.roll(y, LANES - DIFF_HEAD_DIM // 2, axis=1)
            r = y * cos + jnp.where(low, down, up) * sin
            if scale != 1.0:
                r = r * scale
            out_ref[:, c0:c0 + LANES] = r.astype(out_ref.dtype)

    qkv_ref[...] = proj(0).astype(qkv_ref.dtype)
    z_ref[...] = proj(1).astype(z_ref.dtype)
    beta_ref[...] = _sigmoid(proj(2))
    a = proj(3) + dtb_ref[...]
    softplus = jnp.maximum(a, 0.0) + jnp.log(1.0 + jnp.exp(-jnp.abs(a)))
    g_ref[...] = -jnp.exp(alog_ref[...]) * softplus
    rope_store(4, dq_ref, DIFF_HEAD_DIM ** -0.5)
    rope_store(5, dk_ref, 1.0)
    dv_ref[...] = proj(6).astype(dv_ref.dtype)
    rope_store(7, sq_ref, SWA_HEAD_DIM ** -0.5)
    rope_store(8, sk_ref, 1.0)
    sv_ref[...] = proj(9).astype(sv_ref.dtype)


def _inproj(x2d, seq, nw, w_r, cos_t, sin_t, alog_e, dtb_e, tm):
    t, d = x2d.shape
    nseq = seq // tm
    row = lambda i: (i, 0)
    const = lambda i: (0, 0)
    widths = (3 * DN_WIDTH, DN_WIDTH, 2 * DN_WIDTH, 2 * DN_WIDTH, DIFF_WIDTH, DIFF_WIDTH,
              DIFF_WIDTH, SWA_WIDTH, SWA_WIDTH, SWA_WIDTH)
    dtypes = (BF16, BF16, F32, F32, BF16, BF16, BF16, BF16, BF16, BF16)
    return pl.pallas_call(
        _inproj_kernel,
        grid=(t // tm,),
        in_specs=[
            pl.BlockSpec((tm, d), row),
            pl.BlockSpec((1, d), const),
            pl.BlockSpec((d, IN_COLS_R), const, pipeline_mode=pl.Buffered(1)),
            pl.BlockSpec((tm, LANES), lambda i: (i % nseq, 0)),
            pl.BlockSpec((tm, LANES), lambda i: (i % nseq, 0)),
            pl.BlockSpec((1, 2 * DN_WIDTH), const),
            pl.BlockSpec((1, 2 * DN_WIDTH), const),
        ],
        out_specs=[pl.BlockSpec((tm, w), row) for w in widths],
        out_shape=[jax.ShapeDtypeStruct((t, w), dt) for w, dt in zip(widths, dtypes)],
        compiler_params=pltpu.CompilerParams(
            dimension_semantics=("parallel",), vmem_limit_bytes=VMEM_LIMIT),
        name="inproj",
    )(x2d, nw, w_r, cos_t, sin_t, alog_e, dtb_e)


def _block_diag(m, bd_mask):
    return jnp.where(bd_mask, jnp.concatenate([m] * DN_HEADS, axis=0), jnp.zeros((), m.dtype))


def _dn_direction(reverse, x_ref, xp_ref, xn_ref, b_ref, g_ref, cw_ref, o_ref, s_ref,
                  has_prev, has_next, tb):
    c = DN_CHUNK
    w = DN_WIDTH
    xm = x_ref[0].astype(F32)
    xp = jnp.where(has_prev, xp_ref[0].astype(F32), 0.0)
    xn = jnp.where(has_next, xn_ref[0].astype(F32), 0.0)
    ext = jnp.concatenate([xp, xm, xn], axis=0)
    n_ext = tb + 2 * HALO
    cw = cw_ref[...]
    y = jnp.zeros_like(xm)
    for tap in range(DN_CONV):
        off = tap - DN_CONV // 2
        if off == 0:
            sh = xm
        else:
            sh = pltpu.roll(ext, (-off) % n_ext, axis=0)[HALO:HALO + tb]
        y = y + sh * cw[tap:tap + 1, :]
    y = _silu(y)
    q, k, v = y[:, :w], y[:, w:2 * w], y[:, 2 * w:]

    r_i = lax.broadcasted_iota(jnp.int32, (w, w), 0)
    l_i = lax.broadcasted_iota(jnp.int32, (w, w), 1)
    bd_mask = (r_i // DN_HEAD_DIM) == (l_i // DN_HEAD_DIM)
    ones_bd = bd_mask.astype(BF16)

    def seg_sum(t):
        hi = t.astype(BF16)
        lo = (t - hi.astype(F32)).astype(BF16)
        return (jnp.dot(hi, ones_bd, preferred_element_type=F32)
                + jnp.dot(lo, ones_bd, preferred_element_type=F32))

    q = q * lax.rsqrt(seg_sum(q * q) + EPS) * (DN_HEAD_DIM ** -0.5)
    k = k * lax.rsqrt(seg_sum(k * k) + EPS)

    beta = b_ref[0]
    g = g_ref[0]

    i_c = lax.broadcasted_iota(jnp.int32, (c, w), 0)
    j_c = lax.broadcasted_iota(jnp.int32, (c, w), 1) % DN_HEAD_DIM
    if reverse:
        tri, strict, row_tri = i_c <= j_c, i_c < j_c, i_c >= j_c
    else:
        tri, strict, row_tri = i_c >= j_c, i_c > j_c, i_c <= j_c
    eye = (i_c == j_c).astype(F32)

    r_b = lax.broadcasted_iota(jnp.int32, (tb, tb), 0)
    c_b = lax.broadcasted_iota(jnp.int32, (tb, tb), 1)
    same = (r_b // c) == (c_b // c)
    cum = (same & ((r_b <= c_b) if reverse else (r_b >= c_b))).astype(F32)
    gc_all = jnp.dot(cum, g, preferred_element_type=F32, precision=lax.Precision.HIGHEST)

    chunk_ids = range(tb // c)
    if reverse:
        chunk_ids = reversed(chunk_ids)
    for ci in chunk_ids:
        r0 = ci * c
        qc, kc, vc = q[r0:r0 + c], k[r0:r0 + c], v[r0:r0 + c]
        bc, gch, gc = beta[r0:r0 + c], g[r0:r0 + c], gc_all[r0:r0 + c]
        gc_row = jnp.sum(jnp.where(row_tri, gch, 0.0), axis=0, keepdims=True)
        decay = jnp.where(tri, jnp.exp(jnp.where(tri, gc - gc_row, 0.0)), 0.0)
        g_end = gc[0:1] if reverse else gc[c - 1:c]
        eg = jnp.exp(gc)

        kb = kc.astype(BF16)
        gram = lax.dot_general(jnp.concatenate([qc.astype(BF16), kb], axis=0),
                               _block_diag(kb, bd_mask),
                               (((1,), (1,)), ((), ())), preferred_element_type=F32)
        a_intra = gram[:c] * decay
        lmat = jnp.where(strict, gram[c:] * bc * decay, 0.0)

        p = -lmat
        tinv = eye + p
        for _ in range(int(math.log2(c)) - 1):
            pb = p.astype(BF16)
            p = jnp.dot(pb, _block_diag(pb, bd_mask), preferred_element_type=F32)
            tinv = tinv + jnp.dot(tinv.astype(BF16), _block_diag(p.astype(BF16), bd_mask),
                                  preferred_element_type=F32)
        tb16 = tinv.astype(BF16)
        u = jnp.dot(tb16, _block_diag((vc * bc).astype(BF16), bd_mask),
                    preferred_element_type=F32)
        wmat = jnp.dot(tb16, _block_diag((kc * bc * eg).astype(BF16), bd_mask),
                       preferred_element_type=F32)
        q_dec = qc * eg
        k_dec = kc * jnp.exp(g_end - gc)

        state = s_ref[...]
        ws_qs = jnp.dot(jnp.concatenate([wmat, q_dec], axis=0).astype(BF16),
                        state.astype(BF16), preferred_element_type=F32)
        v_new = u - ws_qs[:c]
        vb = v_new.astype(BF16)
        out = ws_qs[c:] + jnp.dot(a_intra.astype(BF16), _block_diag(vb, bd_mask),
                                  preferred_element_type=F32)
        kv = lax.dot_general(k_dec.astype(BF16), vb, (((0,), (0,)), ((), ())),
                             preferred_element_type=F32)
        s_ref[...] = state * jnp.exp(g_end) + jnp.where(bd_mask, kv, 0.0)
        o_ref[0, r0:r0 + c, :] = out


def _deltanet_kernel(xf_ref, xfp_ref, xfn_ref, xb_ref, xbp_ref, xbn_ref,
                     bf_ref, gf_ref, bb_ref, gb_ref, cw_ref,
                     of_ref, ob_ref, sf_ref, sb_ref, *, tb):
    n = pl.program_id(1)
    nb = pl.num_programs(1)

    @pl.when(n == 0)
    def _():
        sf_ref[...] = jnp.zeros_like(sf_ref)
        sb_ref[...] = jnp.zeros_like(sb_ref)

    _dn_direction(False, xf_ref, xfp_ref, xfn_ref, bf_ref, gf_ref, cw_ref, of_ref, sf_ref,
                  n > 0, n < nb - 1, tb)
    _dn_direction(True, xb_ref, xbp_ref, xbn_ref, bb_ref, gb_ref, cw_ref, ob_ref, sb_ref,
                  n < nb - 1, n > 0, tb)


def _deltanet(qkv, beta, g, cw, tb):
    b, s, _ = qkv.shape
    nb = s // tb
    hpb = tb // HALO
    nh = s // HALO
    w3 = 3 * DN_WIDTH
    fwd = lambda bi, n: (bi, n, 0)
    bwd = lambda bi, n: (bi, nb - 1 - n, 0)
    bwd1 = lambda bi, n: (bi, nb - 1 - n, 1)
    fwd_prev = lambda bi, n: (bi, jnp.maximum(n * hpb - 1, 0), 0)
    fwd_next = lambda bi, n: (bi, jnp.minimum((n + 1) * hpb, nh - 1), 0)
    bwd_prev = lambda bi, n: (bi, jnp.maximum((nb - 1 - n) * hpb - 1, 0), 0)
    bwd_next = lambda bi, n: (bi, jnp.minimum((nb - n) * hpb, nh - 1), 0)
    gate = lambda im: pl.BlockSpec((1, tb, DN_WIDTH), im)
    return pl.pallas_call(
        functools.partial(_deltanet_kernel, tb=tb),
        grid=(b, nb),
        in_specs=[
            pl.BlockSpec((1, tb, w3), fwd), pl.BlockSpec((1, HALO, w3), fwd_prev),
            pl.BlockSpec((1, HALO, w3), fwd_next),
            pl.BlockSpec((1, tb, w3), bwd), pl.BlockSpec((1, HALO, w3), bwd_prev),
            pl.BlockSpec((1, HALO, w3), bwd_next),
            gate(fwd), gate(fwd), gate(bwd1), gate(bwd1),
            pl.BlockSpec((8, w3), lambda bi, n: (0, 0)),
        ],
        out_specs=[pl.BlockSpec((1, tb, DN_WIDTH), fwd), pl.BlockSpec((1, tb, DN_WIDTH), bwd)],
        out_shape=[jax.ShapeDtypeStruct((b, s, DN_WIDTH), F32)] * 2,
        scratch_shapes=[pltpu.VMEM((DN_WIDTH, DN_WIDTH), F32)] * 2,
        compiler_params=pltpu.CompilerParams(
            dimension_semantics=("parallel", "arbitrary"), vmem_limit_bytes=VMEM_LIMIT),
        name="deltanet",
    )(qkv, qkv, qkv, qkv, qkv, qkv, beta, g, beta, g, cw)


def _diffattn_kernel(q_ref, k_ref, v_ref, lam_ref, nw_ref, o_ref,
                     q0_ref, q1_ref, m_ref, l_ref, acc_ref, *, lam_init):
    ki = pl.program_id(3)
    d = DIFF_HEAD_DIM

    @pl.when(ki == 0)
    def _():
        q = q_ref[0]
        lane = lax.broadcasted_iota(jnp.int32, q.shape, 1)
        zero = jnp.zeros((), q.dtype)
        q0_ref[...] = jnp.where(lane < d, q, zero)
        q1_ref[...] = jnp.where(lane >= d, q, zero)
        m_ref[...] = jnp.full_like(m_ref, -jnp.inf)
        l_ref[...] = jnp.zeros_like(l_ref)
        acc_ref[...] = jnp.zeros_like(acc_ref)

    k = k_ref[0]
    v = v_ref[0]
    for j, qj_ref in enumerate((q0_ref, q1_ref)):
        s = lax.dot_general(qj_ref[...], k, (((1,), (1,)), ((), ())),
                            preferred_element_type=F32)
        m_prev = m_ref[j]
        m_new = jnp.maximum(m_prev, jnp.max(s, axis=-1, keepdims=True))
        alpha = jnp.exp(m_prev - m_new)
        p = jnp.exp(s - m_new)
        l_ref[j] = alpha * l_ref[j] + jnp.sum(p, axis=-1, keepdims=True)
        acc_ref[j] = alpha * acc_ref[j] + jnp.dot(p.astype(BF16), v,
                                                  preferred_element_type=F32)
        m_ref[j] = m_new

    @pl.when(ki == pl.num_programs(3) - 1)
    def _():
        lv = lam_ref[...]
        lam = (jnp.exp(jnp.sum(lv[0:1] * lv[1:2])) - jnp.exp(jnp.sum(lv[2:3] * lv[3:4]))
               + lam_init)
        o = acc_ref[0] / l_ref[0] - lam * (acc_ref[1] / l_ref[1])
        ms = jnp.mean(o * o, axis=-1, keepdims=True)
        o_ref[0] = (o * lax.rsqrt(ms + EPS) * nw_ref[...] * (1.0 - lam_init)).astype(o_ref.dtype)


def _diffattn(dq, dk, dv, lam_p, nw, lam_init, tq, tk):
    b, s, _ = dq.shape
    hw = 2 * DIFF_HEAD_DIM
    return pl.pallas_call(
        functools.partial(_diffattn_kernel, lam_init=lam_init),
        grid=(b, DIFF_HEADS, s // tq, s // tk),
        in_specs=[
            pl.BlockSpec((1, tq, hw), lambda bi, h, qi, ki: (bi, qi, h)),
            pl.BlockSpec((1, tk, hw), lambda bi, h, qi, ki: (bi, ki, h)),
            pl.BlockSpec((1, tk, hw), lambda bi, h, qi, ki: (bi, ki, h)),
            pl.BlockSpec((4, DIFF_HEAD_DIM), lambda bi, h, qi, ki: (0, 0)),
            pl.BlockSpec((1, hw), lambda bi, h, qi, ki: (0, 0)),
        ],
        out_specs=pl.BlockSpec((1, tq, hw), lambda bi, h, qi, ki: (bi, qi, h)),
        out_shape=jax.ShapeDtypeStruct((b, s, DIFF_WIDTH), BF16),
        scratch_shapes=[pltpu.VMEM((tq, hw), BF16), pltpu.VMEM((tq, hw), BF16),
                        pltpu.VMEM((2, tq, 1), F32), pltpu.VMEM((2, tq, 1), F32),
                        pltpu.VMEM((2, tq, hw), F32)],
        compiler_params=pltpu.CompilerParams(
            dimension_semantics=("parallel", "parallel", "parallel", "arbitrary"),
            vmem_limit_bytes=VMEM_LIMIT),
        name="diffattn",
    )(dq, dk, dv, lam_p, nw)


def _swa_kernel(q_ref, kp_ref, km_ref, kn_ref, vp_ref, vm_ref, vn_ref, sink_ref, o_ref, *, tq):
    qi = pl.program_id(1)
    g = pl.program_id(2)
    nq = pl.num_programs(1)
    d = SWA_HEAD_DIM
    grp = SWA_HEADS // SWA_KV_HEADS
    q = q_ref[0]
    k = jnp.concatenate([kp_ref[0], km_ref[0], kn_ref[0]], axis=0)
    v = jnp.concatenate([vp_ref[0], vm_ref[0], vn_ref[0]], axis=0)
    nk = tq + 2 * WINDOW
    qpos = lax.broadcasted_iota(jnp.int32, (tq, nk), 0)
    koff = lax.broadcasted_iota(jnp.int32, (tq, nk), 1) - WINDOW
    valid = jnp.abs(koff - qpos) <= WINDOW
    valid &= (koff >= 0) | (qi > 0)
    valid &= (koff < tq) | (qi < nq - 1)
    lane = lax.broadcasted_iota(jnp.int32, q.shape, 1)
    zero = jnp.zeros((), q.dtype)
    out = jnp.zeros((tq, grp * d), F32)
    for j in range(grp):
        in_head = (lane >= j * d) & (lane < (j + 1) * d)
        s = lax.dot_general(jnp.where(in_head, q, zero), k, (((1,), (1,)), ((), ())),
                            preferred_element_type=F32)
        s = jnp.where(valid, s, -jnp.inf)
        sink = sink_ref[g * grp + j]
        m = jnp.maximum(jnp.max(s, axis=-1, keepdims=True), sink)
        p = jnp.exp(s - m)
        denom = jnp.sum(p, axis=-1, keepdims=True) + jnp.exp(sink - m)
        pv = jnp.dot(p.astype(BF16), v, preferred_element_type=F32)
        out = jnp.where(in_head, pv / denom, out)
    o_ref[0] = out.astype(o_ref.dtype)


def _swa(sq, sk, sv, sink, tq):
    b, s, _ = sq.shape
    nq = s // tq
    wpb = tq // WINDOW
    nw = s // WINDOW
    gw = (SWA_HEADS // SWA_KV_HEADS) * SWA_HEAD_DIM
    main = lambda bi, qi, g: (bi, qi, g)
    prev = lambda bi, qi, g: (bi, jnp.maximum(qi * wpb - 1, 0), g)
    nxt = lambda bi, qi, g: (bi, jnp.minimum((qi + 1) * wpb, nw - 1), g)
    return pl.pallas_call(
        functools.partial(_swa_kernel, tq=tq),
        grid=(b, nq, SWA_KV_HEADS),
        in_specs=[
            pl.BlockSpec((1, tq, gw), main),
            pl.BlockSpec((1, WINDOW, gw), prev), pl.BlockSpec((1, tq, gw), main),
            pl.BlockSpec((1, WINDOW, gw), nxt),
            pl.BlockSpec((1, WINDOW, gw), prev), pl.BlockSpec((1, tq, gw), main),
            pl.BlockSpec((1, WINDOW, gw), nxt),
            pl.BlockSpec(memory_space=pltpu.SMEM),
        ],
        out_specs=pl.BlockSpec((1, tq, gw), main),
        out_shape=jax.ShapeDtypeStruct((b, s, SWA_WIDTH), BF16),
        compiler_params=pltpu.CompilerParams(
            dimension_semantics=("parallel", "parallel", "parallel"),
            vmem_limit_bytes=VMEM_LIMIT),
        name="swa",
    )(sq, sk, sk, sk, sv, sv, sv, sink)


def _outmlp_kernel(x_ref, of_ref, ob_ref, z_ref, df_ref, sw_ref, dnw_ref, wo_ref,
                   n2_ref, wu_ref, wd_ref, fn_ref, y_ref, *, final_norm, ff_chunk):
    w = DN_WIDTH
    o = of_ref[...] + ob_ref[...]
    r_i = lax.broadcasted_iota(jnp.int32, (w, w), 0)
    l_i = lax.broadcasted_iota(jnp.int32, (w, w), 1)
    ones_bd = ((r_i // DN_HEAD_DIM) == (l_i // DN_HEAD_DIM)).astype(BF16)
    sq = o * o
    hi = sq.astype(BF16)
    lo = (sq - hi.astype(F32)).astype(BF16)
    ss = (jnp.dot(hi, ones_bd, preferred_element_type=F32)
          + jnp.dot(lo, ones_bd, preferred_element_type=F32))
    dn = o * lax.rsqrt(ss * (1.0 / DN_HEAD_DIM) + EPS) * dnw_ref[...]
    dn = dn * _silu(z_ref[...].astype(F32))
    x = x_ref[...]
    x = x + jnp.dot(dn.astype(BF16), wo_ref[0:w, :], preferred_element_type=F32)
    x = x + jnp.dot(df_ref[...], wo_ref[w:w + DIFF_WIDTH, :], preferred_element_type=F32)
    x = x + jnp.dot(sw_ref[...], wo_ref[w + DIFF_WIDTH:, :], preferred_element_type=F32)
    ms = jnp.mean(x * x, axis=-1, keepdims=True)
    hn = (x * lax.rsqrt(ms + EPS) * n2_ref[...]).astype(BF16)
    mlp = jnp.zeros_like(x)
    for c0 in range(0, D_FF, ff_chunk):
        h = jnp.dot(hn, wu_ref[:, c0:c0 + ff_chunk], preferred_element_type=F32)
        h = jnp.square(jnp.maximum(h, 0.0)).astype(BF16)
        mlp = mlp + jnp.dot(h, wd_ref[c0:c0 + ff_chunk, :], preferred_element_type=F32)
    x = x + mlp
    if final_norm:
        ms = jnp.mean(x * x, axis=-1, keepdims=True)
        x = x * lax.rsqrt(ms + EPS) * fn_ref[...]
    y_ref[...] = x


def _outmlp(x2d, o_f, o_b, z, df, sw, dnw_e, wo, n2, wu, wd, fn, final_norm, tm):
    t, d = x2d.shape
    row = lambda i: (i, 0)
    const = lambda i: (0, 0)
    resident = lambda shape: pl.BlockSpec(shape, const)
    return pl.pallas_call(
        functools.partial(_outmlp_kernel, final_norm=final_norm, ff_chunk=1024),
        grid=(t // tm,),
        in_specs=[
            pl.BlockSpec((tm, d), row),
            pl.BlockSpec((tm, DN_WIDTH), row), pl.BlockSpec((tm, DN_WIDTH), row),
            pl.BlockSpec((tm, DN_WIDTH), row), pl.BlockSpec((tm, DIFF_WIDTH), row),
            pl.BlockSpec((tm, SWA_WIDTH), row),
            pl.BlockSpec((1, DN_WIDTH), const),
            resident((d, d)), pl.BlockSpec((1, d), const),
            resident((d, D_FF)), resident((D_FF, d)),
            pl.BlockSpec((1, d), const),
        ],
        out_specs=pl.BlockSpec((tm, d), row),
        out_shape=jax.ShapeDtypeStruct((t, d), F32),
        compiler_params=pltpu.CompilerParams(
            dimension_semantics=("parallel",), vmem_limit_bytes=VMEM_LIMIT),
        name="outmlp",
    )(x2d, o_f, o_b, z, df, sw, dnw_e, wo, n2, wu, wd, fn)


def _pick(n, pref):
    t = min(n, pref)
    assert n % t == 0, (n, pref)
    return t


def _trunk(x, params):
    b, s, d = x.shape
    assert d == D_MODEL and s % DN_CHUNK == 0 and s % WINDOW == 0
    depth = params["w_in_r"].shape[0]
    tm = _pick(s, 512)
    cos_t, sin_t = _rope_tables(s)
    x2d = x.reshape(b * s, d)
    for li in range(depth):
        p = {k: v[li] for k, v in params.items() if k != "final_norm_w"}
        lam_init = 0.8 - 0.6 * math.exp(-0.3 * li)
        qkv, z, beta, g, dq, dk, dv, sq, sk, sv = _inproj(
            x2d, s, p["norm1_w"], p["w_in_r"], cos_t, sin_t, p["alog_e"], p["dtb_e"], tm)
        r3 = lambda a: a.reshape(b, s, a.shape[-1])
        o_f, o_b = _deltanet(r3(qkv), r3(beta), r3(g), p["conv_w"], _pick(s, 256))
        df = _diffattn(r3(dq), r3(dk), r3(dv), p["diff_lambda"], p["diff_norm_w"], lam_init,
                       _pick(s, 512), _pick(s, 512))
        sw = _swa(r3(sq), r3(sk), r3(sv), p["swa_sink"], _pick(s, 256))
        r2 = lambda a: a.reshape(b * s, a.shape[-1])
        x2d = _outmlp(x2d, r2(o_f), r2(o_b), z, r2(df), r2(sw), p["dn_norm_e"], p["w_out"],
                      p["norm2_w"], p["w_up"], p["w_down"], params["final_norm_w"],
                      li == depth - 1, tm)
    return x2d.reshape(b, s, d)


def kernel(x_prompt, x_sample, norm1_w, w_in, dn_conv_w, dn_a_log, dn_dt_bias, dn_norm_w,
           diff_lambda, diff_norm_w, swa_sink, w_out, norm2_w, w_up, w_down, final_norm_w):
    depth = w_in.shape[0]
    rep = lambda a: jnp.repeat(a.astype(F32).reshape(depth, 1, 2 * DN_HEADS), DN_HEAD_DIM, axis=2)
    params = {
        "norm1_w": norm1_w.astype(F32)[:, None, :],
        "w_in_r": jnp.take(w_in, jnp.asarray(_in_col_index()), axis=2).astype(BF16),
        "conv_w": jnp.pad(dn_conv_w.astype(F32), ((0, 0), (0, 8 - DN_CONV), (0, 0))),
        "alog_e": rep(dn_a_log),
        "dtb_e": rep(dn_dt_bias),
        "dn_norm_e": jnp.tile(dn_norm_w.astype(F32), (1, DN_HEADS))[:, None, :],
        "diff_lambda": diff_lambda.astype(F32),
        "diff_norm_w": diff_norm_w.astype(F32)[:, None, :],
        "swa_sink": swa_sink.astype(F32),
        "w_out": w_out.astype(BF16),
        "norm2_w": norm2_w.astype(F32)[:, None, :],
        "w_up": w_up.astype(BF16),
        "w_down": w_down.astype(BF16),
        "final_norm_w": final_norm_w.astype(F32)[None, :],
    }
    return _trunk(x_prompt, params), _trunk(x_sample, params)
```

```python
import functools
import math

import numpy as np
import jax
import jax.numpy as jnp
from jax import lax
from jax.experimental import pallas as pl
from jax.experimental.pallas import tpu as pltpu

F32 = jnp.float32
BF16 = jnp.bfloat16

D_MODEL = 1024
DN_HEADS = 4
DN_HEAD_DIM = 64
DN_WIDTH = DN_HEADS * DN_HEAD_DIM
DN_CONV = 5
DN_CHUNK = 64
DIFF_HEADS = 4
DIFF_HEAD_DIM = 64
DIFF_WIDTH = DIFF_HEADS * 2 * DIFF_HEAD_DIM
SWA_HEADS = 4
SWA_KV_HEADS = 2
SWA_HEAD_DIM = 64
SWA_WIDTH = SWA_HEADS * SWA_HEAD_DIM
WINDOW = 128
ROPE_THETA = 10000.0
D_FF = 4 * D_MODEL
EPS = 1e-6

LANES = 128
HALO = 16
ONES_ROWS = 16
LOG2E = math.log2(math.e)
VMEM_LIMIT = 56 * 1024 * 1024

_SEC_SIZES = (3 * DN_WIDTH, DN_WIDTH, 2 * DN_WIDTH, 2 * DN_WIDTH,
              DIFF_WIDTH, DIFF_WIDTH, DIFF_WIDTH, SWA_WIDTH, SWA_WIDTH, SWA_WIDTH)
_SEC_OFFS = tuple(int(v) for v in np.cumsum((0,) + _SEC_SIZES))
IN_COLS_R = _SEC_OFFS[-1]


def _in_col_index():
    o_z = 3 * DN_WIDTH
    o_b = o_z + DN_WIDTH
    o_a = o_b + 2 * DN_HEADS
    o_dq = o_a + 2 * DN_HEADS
    o_dk = o_dq + DIFF_WIDTH
    o_dv = o_dk + DIFF_WIDTH
    o_sq = o_dv + DIFF_WIDTH
    o_sk = o_sq + SWA_WIDTH
    o_sv = o_sk + SWA_KV_HEADS * SWA_HEAD_DIM
    grp = SWA_HEADS // SWA_KV_HEADS
    kv_rep = np.concatenate([np.tile(np.arange(SWA_HEAD_DIM) + g * SWA_HEAD_DIM, grp)
                             for g in range(SWA_KV_HEADS)])
    idx = np.concatenate([
        np.arange(0, o_z), np.arange(o_z, o_b),
        np.repeat(np.arange(o_b, o_a), DN_HEAD_DIM),
        np.repeat(np.arange(o_a, o_dq), DN_HEAD_DIM),
        np.arange(o_dq, o_dk), np.arange(o_dk, o_dv), np.arange(o_dv, o_sq),
        np.arange(o_sq, o_sk), o_sk + kv_rep, o_sv + kv_rep])
    assert idx.shape[0] == IN_COLS_R
    return idx.astype(np.int32)


def _rope_tables(seq):
    half = DIFF_HEAD_DIM // 2
    inv = ROPE_THETA ** (-jnp.arange(half, dtype=F32) / half)
    ang = jnp.arange(seq, dtype=F32)[:, None] * inv[None, :]
    cos, sin = jnp.cos(ang), jnp.sin(ang)
    reps = LANES // (2 * half)
    cos_t = jnp.tile(jnp.concatenate([cos, cos], axis=1), (1, reps))
    sin_t = jnp.tile(jnp.concatenate([-sin, sin], axis=1), (1, reps))
    return cos_t, sin_t


def _sigmoid(x):
    return 1.0 / (1.0 + jnp.exp(-x))


def _silu(x):
    return x * _sigmoid(x)


def _inproj_kernel(x_ref, nw_ref, w_ref, cos_ref, sin_ref, alog_ref, dtb_ref,
                   qkv_ref, z_ref, beta_ref, g_ref, dq_ref, dk_ref, dv_ref,
                   sq_ref, sk_ref, sv_ref):
    x = x_ref[...]
    ms = jnp.mean(x * x, axis=-1, keepdims=True)
    hn = (x * lax.rsqrt(ms + EPS) * nw_ref[...]).astype(BF16)
    cos = cos_ref[...]
    sin = sin_ref[...]
    lane = lax.broadcasted_iota(jnp.int32, cos.shape, 1)
    low = (lane % DIFF_HEAD_DIM) < (DIFF_HEAD_DIM // 2)

    def proj(sec):
        return jnp.dot(hn, w_ref[:, _SEC_OFFS[sec]:_SEC_OFFS[sec + 1]],
                       preferred_element_type=F32)

    def rope_store(sec, out_ref, scale):
        y_all = proj(sec)
        for c0 in range(0, _SEC_SIZES[sec], LANES):
            y = y_all[:, c0:c0 + LANES]
            up = pltpu.roll(y, DIFF_HEAD_DIM // 2, axis=1)
            down = pltpu.roll(y, LANES - DIFF_HEAD_DIM // 2, axis=1)
            r = y * cos + jnp.where(low, down, up) * sin
            if scale != 1.0:
                r = r * scale
            out_ref[:, c0:c0 + LANES] = r.astype(out_ref.dtype)

    qkv_ref[...] = proj(0).astype(qkv_ref.dtype)
    z_ref[...] = proj(1).astype(z_ref.dtype)
    beta_ref[...] = _sigmoid(proj(2))
    a = proj(3) + dtb_ref[...]
    softplus = jnp.maximum(a, 0.0) + jnp.log(1.0 + jnp.exp(-jnp.abs(a)))
    g_ref[...] = -jnp.exp(alog_ref[...]) * softplus
    rope_store(4, dq_ref, DIFF_HEAD_DIM ** -0.5 * LOG2E)
    rope_store(5, dk_ref, 1.0)
    dv_ref[...] = proj(6).astype(dv_ref.dtype)
    rope_store(7, sq_ref, SWA_HEAD_DIM ** -0.5)
    rope_store(8, sk_ref, 1.0)
    sv_ref[...] = proj(9).astype(sv_ref.dtype)


def _inproj(x2d, seq, nw, w_r, cos_t, sin_t, alog_e, dtb_e, tm):
    t, d = x2d.shape
    nseq = seq // tm
    row = lambda i: (i, 0)
    const = lambda i: (0, 0)
    widths = (3 * DN_WIDTH, DN_WIDTH, 2 * DN_WIDTH, 2 * DN_WIDTH, DIFF_WIDTH, DIFF_WIDTH,
              DIFF_WIDTH, SWA_WIDTH, SWA_WIDTH, SWA_WIDTH)
    dtypes = (BF16, BF16, F32, F32, BF16, BF16, BF16, BF16, BF16, BF16)
    return pl.pallas_call(
        _inproj_kernel,
        grid=(t // tm,),
        in_specs=[
            pl.BlockSpec((tm, d), row),
            pl.BlockSpec((1, d), const),
            pl.BlockSpec((d, IN_COLS_R), const, pipeline_mode=pl.Buffered(1)),
            pl.BlockSpec((tm, LANES), lambda i: (i % nseq, 0)),
            pl.BlockSpec((tm, LANES), lambda i: (i % nseq, 0)),
            pl.BlockSpec((1, 2 * DN_WIDTH), const),
            pl.BlockSpec((1, 2 * DN_WIDTH), const),
        ],
        out_specs=[pl.BlockSpec((tm, w), row) for w in widths],
        out_shape=[jax.ShapeDtypeStruct((t, w), dt) for w, dt in zip(widths, dtypes)],
        compiler_params=pltpu.CompilerParams(
            dimension_semantics=("parallel",), vmem_limit_bytes=VMEM_LIMIT),
        name="inproj",
    )(x2d, nw, w_r, cos_t, sin_t, alog_e, dtb_e)


def _block_diag(m, bd_mask):
    return jnp.where(bd_mask, jnp.concatenate([m] * DN_HEADS, axis=0), jnp.zeros((), m.dtype))


def _dn_direction(reverse, x_ref, xp_ref, xn_ref, b_ref, g_ref, cw_ref, o_ref, s_ref,
                  has_prev, has_next, tb):
    c = DN_CHUNK
    w = DN_WIDTH
    xm = x_ref[0].astype(F32)
    xp = jnp.where(has_prev, xp_ref[0].astype(F32), 0.0)
    xn = jnp.where(has_next, xn_ref[0].astype(F32), 0.0)
    ext = jnp.concatenate([xp, xm, xn], axis=0)
    n_ext = tb + 2 * HALO
    cw = cw_ref[...]
    y = jnp.zeros_like(xm)
    for tap in range(DN_CONV):
        off = tap - DN_CONV // 2
        if off == 0:
            sh = xm
        else:
            sh = pltpu.roll(ext, (-off) % n_ext, axis=0)[HALO:HALO + tb]
        y = y + sh * cw[tap:tap + 1, :]
    y = _silu(y)
    q, k, v = y[:, :w], y[:, w:2 * w], y[:, 2 * w:]

    r_i = lax.broadcasted_iota(jnp.int32, (w, w), 0)
    l_i = lax.broadcasted_iota(jnp.int32, (w, w), 1)
    bd_mask = (r_i // DN_HEAD_DIM) == (l_i // DN_HEAD_DIM)
    ones_bd = bd_mask.astype(BF16)

    def seg_sum(t):
        hi = t.astype(BF16)
        lo = (t - hi.astype(F32)).astype(BF16)
        return (jnp.dot(hi, ones_bd, preferred_element_type=F32)
                + jnp.dot(lo, ones_bd, preferred_element_type=F32))

    q = q * lax.rsqrt(seg_sum(q * q) + EPS) * (DN_HEAD_DIM ** -0.5)
    k = k * lax.rsqrt(seg_sum(k * k) + EPS)

    beta = b_ref[0]
    g = g_ref[0]

    i_c = lax.broadcasted_iota(jnp.int32, (c, w), 0)
    j_c = lax.broadcasted_iota(jnp.int32, (c, w), 1) % DN_HEAD_DIM
    if reverse:
        tri, strict, row_tri = i_c <= j_c, i_c < j_c, i_c >= j_c
    else:
        tri, strict, row_tri = i_c >= j_c, i_c > j_c, i_c <= j_c
    eye = (i_c == j_c).astype(F32)
    lvl_masks = []
    for lvl in range(int(math.log2(c))):
        siblings = (i_c >> (lvl + 1)) == (j_c >> (lvl + 1))
        odd_i = ((i_c >> lvl) & 1) == 1
        odd_j = ((j_c >> lvl) & 1) == 1
        lvl_masks.append(siblings & ((~odd_i & odd_j) if reverse else (odd_i & ~odd_j)))

    r_b = lax.broadcasted_iota(jnp.int32, (tb, tb), 0)
    c_b = lax.broadcasted_iota(jnp.int32, (tb, tb), 1)
    same = (r_b // c) == (c_b // c)
    cum = (same & ((r_b <= c_b) if reverse else (r_b >= c_b))).astype(F32)
    gc_all = jnp.dot(cum, g, preferred_element_type=F32, precision=lax.Precision.HIGHEST)

    chunk_ids = range(tb // c)
    if reverse:
        chunk_ids = reversed(chunk_ids)
    for ci in chunk_ids:
        r0 = ci * c
        qc, kc, vc = q[r0:r0 + c], k[r0:r0 + c], v[r0:r0 + c]
        bc, gch, gc = beta[r0:r0 + c], g[r0:r0 + c], gc_all[r0:r0 + c]
        gc_row = jnp.sum(jnp.where(row_tri, gch, 0.0), axis=0, keepdims=True)
        decay = jnp.where(tri, jnp.exp(jnp.where(tri, gc - gc_row, 0.0)), 0.0)
        g_end = gc[0:1] if reverse else gc[c - 1:c]
        eg = jnp.exp(gc)

        kb = kc.astype(BF16)
        gram = lax.dot_general(jnp.concatenate([qc.astype(BF16), kb], axis=0),
                               _block_diag(kb, bd_mask),
                               (((1,), (1,)), ((), ())), preferred_element_type=F32)
        a_intra = gram[:c] * decay
        lmat = jnp.where(strict, gram[c:] * bc * decay, 0.0)

        tinv = eye - jnp.where(lvl_masks[0], lmat, 0.0)
        for lvl_mask in lvl_masks[1:]:
            loff = jnp.where(lvl_mask, lmat, 0.0).astype(BF16)
            m1 = jnp.dot(loff, _block_diag(tinv.astype(BF16), bd_mask),
                         preferred_element_type=F32)
            tinv = tinv - jnp.dot(tinv.astype(BF16), _block_diag(m1.astype(BF16), bd_mask),
                                  preferred_element_type=F32)
        tb16 = tinv.astype(BF16)
        u = jnp.dot(tb16, _block_diag((vc * bc).astype(BF16), bd_mask),
                    preferred_element_type=F32)
        wmat = jnp.dot(tb16, _block_diag((kc * bc * eg).astype(BF16), bd_mask),
                       preferred_element_type=F32)
        q_dec = qc * eg
        k_dec = kc * jnp.exp(g_end - gc)

        state = s_ref[...]
        ws_qs = jnp.dot(jnp.concatenate([wmat, q_dec], axis=0).astype(BF16),
                        state.astype(BF16), preferred_element_type=F32)
        v_new = u - ws_qs[:c]
        vb = v_new.astype(BF16)
        out = ws_qs[c:] + jnp.dot(a_intra.astype(BF16), _block_diag(vb, bd_mask),
                                  preferred_element_type=F32)
        kv = lax.dot_general(k_dec.astype(BF16), vb, (((0,), (0,)), ((), ())),
                             preferred_element_type=F32)
        s_ref[...] = state * jnp.exp(g_end) + jnp.where(bd_mask, kv, 0.0)
        o_ref[0, r0:r0 + c, :] = out


def _deltanet_kernel(xf_ref, xfp_ref, xfn_ref, xb_ref, xbp_ref, xbn_ref,
                     bf_ref, gf_ref, bb_ref, gb_ref, cw_ref,
                     of_ref, ob_ref, sf_ref, sb_ref, *, tb):
    n = pl.program_id(1)
    nb = pl.num_programs(1)

    @pl.when(n == 0)
    def _():
        sf_ref[...] = jnp.zeros_like(sf_ref)
        sb_ref[...] = jnp.zeros_like(sb_ref)

    _dn_direction(False, xf_ref, xfp_ref, xfn_ref, bf_ref, gf_ref, cw_ref, of_ref, sf_ref,
                  n > 0, n < nb - 1, tb)
    _dn_direction(True, xb_ref, xbp_ref, xbn_ref, bb_ref, gb_ref, cw_ref, ob_ref, sb_ref,
                  n < nb - 1, n > 0, tb)


def _deltanet(qkv, beta, g, cw, tb):
    b, s, _ = qkv.shape
    nb = s // tb
    hpb = tb // HALO
    nh = s // HALO
    w3 = 3 * DN_WIDTH
    fwd = lambda bi, n: (bi, n, 0)
    bwd = lambda bi, n: (bi, nb - 1 - n, 0)
    bwd1 = lambda bi, n: (bi, nb - 1 - n, 1)
    fwd_prev = lambda bi, n: (bi, jnp.maximum(n * hpb - 1, 0), 0)
    fwd_next = lambda bi, n: (bi, jnp.minimum((n + 1) * hpb, nh - 1), 0)
    bwd_prev = lambda bi, n: (bi, jnp.maximum((nb - 1 - n) * hpb - 1, 0), 0)
    bwd_next = lambda bi, n: (bi, jnp.minimum((nb - n) * hpb, nh - 1), 0)
    gate = lambda im: pl.BlockSpec((1, tb, DN_WIDTH), im)
    return pl.pallas_call(
        functools.partial(_deltanet_kernel, tb=tb),
        grid=(b, nb),
        in_specs=[
            pl.BlockSpec((1, tb, w3), fwd), pl.BlockSpec((1, HALO, w3), fwd_prev),
            pl.BlockSpec((1, HALO, w3), fwd_next),
            pl.BlockSpec((1, tb, w3), bwd), pl.BlockSpec((1, HALO, w3), bwd_prev),
            pl.BlockSpec((1, HALO, w3), bwd_next),
            gate(fwd), gate(fwd), gate(bwd1), gate(bwd1),
            pl.BlockSpec((8, w3), lambda bi, n: (0, 0)),
        ],
        out_specs=[pl.BlockSpec((1, tb, DN_WIDTH), fwd), pl.BlockSpec((1, tb, DN_WIDTH), bwd)],
        out_shape=[jax.ShapeDtypeStruct((b, s, DN_WIDTH), F32)] * 2,
        scratch_shapes=[pltpu.VMEM((DN_WIDTH, DN_WIDTH), F32)] * 2,
        compiler_params=pltpu.CompilerParams(
            dimension_semantics=("parallel", "arbitrary"), vmem_limit_bytes=VMEM_LIMIT),
        name="deltanet",
    )(qkv, qkv, qkv, qkv, qkv, qkv, beta, g, beta, g, cw)


def _diffattn_kernel(q_ref, k_ref, vt_ref, lam_ref, nw_ref, o_ref,
                     q0_ref, q1_ref, m_ref, acc_ref, *, lam_init, tks):
    ki = pl.program_id(3)
    d = DIFF_HEAD_DIM
    dv = 2 * d

    @pl.when(ki == 0)
    def _():
        q = q_ref[0]
        lane = lax.broadcasted_iota(jnp.int32, q.shape, 1)
        zero = jnp.zeros((), q.dtype)
        q0_ref[...] = jnp.where(lane < d, q, zero)
        q1_ref[...] = jnp.where(lane >= d, q, zero)
        m_ref[...] = jnp.full_like(m_ref, -jnp.inf)
        acc_ref[...] = jnp.zeros_like(acc_ref)

    tk = k_ref.shape[1]
    ones = jnp.ones((ONES_ROWS, tks), BF16)
    q_refs = (q0_ref, q1_ref)

    def scores(sub, j):
        kb = k_ref[0, sub * tks:(sub + 1) * tks, :]
        return lax.dot_general(kb, q_refs[j][...], (((1,), (1,)), ((), ())),
                               preferred_element_type=F32)

    chains = [(sub, j) for sub in range(tk // tks) for j in range(2)]
    m = [m_ref[0], m_ref[1]]
    acc = [acc_ref[0], acc_ref[1]]
    s_next = scores(*chains[0])
    for idx, (sub, j) in enumerate(chains):
        s = s_next
        if idx + 1 < len(chains):
            s_next = scores(*chains[idx + 1])
        vt = jnp.concatenate([vt_ref[0, 0, :, sub * tks:(sub + 1) * tks], ones], axis=0)
        m_new = jnp.maximum(m[j], jnp.max(s, axis=0, keepdims=True))
        alpha = jnp.exp2(m[j] - m_new)
        p = jnp.exp2(s - m_new).astype(BF16)
        acc[j] = alpha * acc[j] + jnp.dot(vt, p, preferred_element_type=F32)
        m[j] = m_new
    for j in range(2):
        m_ref[j] = m[j]
        acc_ref[j] = acc[j]

    @pl.when(ki == pl.num_programs(3) - 1)
    def _():
        lv = lam_ref[...]
        lam = (jnp.exp(jnp.sum(lv[0:1] * lv[1:2])) - jnp.exp(jnp.sum(lv[2:3] * lv[3:4]))
               + lam_init)
        a0 = acc_ref[0]
        a1 = acc_ref[1]
        o = a0[:dv] / a0[dv:dv + 1] - lam * (a1[:dv] / a1[dv:dv + 1])
        ms = jnp.mean(o * o, axis=0, keepdims=True)
        o = o * lax.rsqrt(ms + EPS) * nw_ref[...] * (1.0 - lam_init)
        o_ref[0] = o.T.astype(o_ref.dtype)


def _diffattn(dq, dk, dvt, lam_p, nw_col, lam_init, tq, tk, tks):
    b, s, _ = dq.shape
    hw = 2 * DIFF_HEAD_DIM
    return pl.pallas_call(
        functools.partial(_diffattn_kernel, lam_init=lam_init, tks=tks),
        grid=(b, DIFF_HEADS, s // tq, s // tk),
        in_specs=[
            pl.BlockSpec((1, tq, hw), lambda bi, h, qi, ki: (bi, qi, h)),
            pl.BlockSpec((1, tk, hw), lambda bi, h, qi, ki: (bi, ki, h)),
            pl.BlockSpec((1, 1, hw, tk), lambda bi, h, qi, ki: (bi, h, 0, ki)),
            pl.BlockSpec((4, DIFF_HEAD_DIM), lambda bi, h, qi, ki: (0, 0)),
            pl.BlockSpec((hw, 1), lambda bi, h, qi, ki: (0, 0)),
        ],
        out_specs=pl.BlockSpec((1, tq, hw), lambda bi, h, qi, ki: (bi, qi, h)),
        out_shape=jax.ShapeDtypeStruct((b, s, DIFF_WIDTH), BF16),
        scratch_shapes=[pltpu.VMEM((tq, hw), BF16), pltpu.VMEM((tq, hw), BF16),
                        pltpu.VMEM((2, 1, tq), F32),
                        pltpu.VMEM((2, hw + ONES_ROWS, tq), F32)],
        compiler_params=pltpu.CompilerParams(
            dimension_semantics=("parallel", "parallel", "parallel", "arbitrary"),
            vmem_limit_bytes=VMEM_LIMIT),
        name="diffattn",
    )(dq, dk, dvt, lam_p, nw_col)


def _swa_kernel(q_ref, kp_ref, km_ref, kn_ref, vp_ref, vm_ref, vn_ref, sink_ref, o_ref, *, tq):
    qi = pl.program_id(1)
    g = pl.program_id(2)
    nq = pl.num_programs(1)
    d = SWA_HEAD_DIM
    grp = SWA_HEADS // SWA_KV_HEADS
    q = q_ref[0]
    k = jnp.concatenate([kp_ref[0], km_ref[0], kn_ref[0]], axis=0)
    v = jnp.concatenate([vp_ref[0], vm_ref[0], vn_ref[0]], axis=0)
    nk = tq + 2 * WINDOW
    qpos = lax.broadcasted_iota(jnp.int32, (tq, nk), 0)
    koff = lax.broadcasted_iota(jnp.int32, (tq, nk), 1) - WINDOW
    valid = jnp.abs(koff - qpos) <= WINDOW
    valid &= (koff >= 0) | (qi > 0)
    valid &= (koff < tq) | (qi < nq - 1)
    lane = lax.broadcasted_iota(jnp.int32, q.shape, 1)
    zero = jnp.zeros((), q.dtype)
    out = jnp.zeros((tq, grp * d), F32)
    for j in range(grp):
        in_head = (lane >= j * d) & (lane < (j + 1) * d)
        s = lax.dot_general(jnp.where(in_head, q, zero), k, (((1,), (1,)), ((), ())),
                            preferred_element_type=F32)
        s = jnp.where(valid, s, -jnp.inf)
        sink = sink_ref[g * grp + j]
        m = jnp.maximum(jnp.max(s, axis=-1, keepdims=True), sink)
        p = jnp.exp(s - m)
        denom = jnp.sum(p, axis=-1, keepdims=True) + jnp.exp(sink - m)
        pv = jnp.dot(p.astype(BF16), v, preferred_element_type=F32)
        out = jnp.where(in_head, pv / denom, out)
    o_ref[0] = out.astype(o_ref.dtype)


def _swa(sq, sk, sv, sink, tq):
    b, s, _ = sq.shape
    nq = s // tq
    wpb = tq // WINDOW
    nw = s // WINDOW
    gw = (SWA_HEADS // SWA_KV_HEADS) * SWA_HEAD_DIM
    main = lambda bi, qi, g: (bi, qi, g)
    prev = lambda bi, qi, g: (bi, jnp.maximum(qi * wpb - 1, 0), g)
    nxt = lambda bi, qi, g: (bi, jnp.minimum((qi + 1) * wpb, nw - 1), g)
    return pl.pallas_call(
        functools.partial(_swa_kernel, tq=tq),
        grid=(b, nq, SWA_KV_HEADS),
        in_specs=[
            pl.BlockSpec((1, tq, gw), main),
            pl.BlockSpec((1, WINDOW, gw), prev), pl.BlockSpec((1, tq, gw), main),
            pl.BlockSpec((1, WINDOW, gw), nxt),
            pl.BlockSpec((1, WINDOW, gw), prev), pl.BlockSpec((1, tq, gw), main),
            pl.BlockSpec((1, WINDOW, gw), nxt),
            pl.BlockSpec(memory_space=pltpu.SMEM),
        ],
        out_specs=pl.BlockSpec((1, tq, gw), main),
        out_shape=jax.ShapeDtypeStruct((b, s, SWA_WIDTH), BF16),
        compiler_params=pltpu.CompilerParams(
            dimension_semantics=("parallel", "parallel", "parallel"),
            vmem_limit_bytes=VMEM_LIMIT),
        name="swa",
    )(sq, sk, sk, sk, sv, sv, sv, sink)


def _outmlp_kernel(x_ref, of_ref, ob_ref, z_ref, df_ref, sw_ref, dnw_ref, wo_ref,
                   n2_ref, wu_ref, wd_ref, fn_ref, y_ref, *, final_norm, ff_chunk):
    w = DN_WIDTH
    o = of_ref[...] + ob_ref[...]
    r_i = lax.broadcasted_iota(jnp.int32, (w, w), 0)
    l_i = lax.broadcasted_iota(jnp.int32, (w, w), 1)
    ones_bd = ((r_i // DN_HEAD_DIM) == (l_i // DN_HEAD_DIM)).astype(BF16)
    sq = o * o
    hi = sq.astype(BF16)
    lo = (sq - hi.astype(F32)).astype(BF16)
    ss = (jnp.dot(hi, ones_bd, preferred_element_type=F32)
          + jnp.dot(lo, ones_bd, preferred_element_type=F32))
    dn = o * lax.rsqrt(ss * (1.0 / DN_HEAD_DIM) + EPS) * dnw_ref[...]
    dn = dn * _silu(z_ref[...].astype(F32))
    x = x_ref[...]
    x = x + jnp.dot(dn.astype(BF16), wo_ref[0:w, :], preferred_element_type=F32)
    x = x + jnp.dot(df_ref[...], wo_ref[w:w + DIFF_WIDTH, :], preferred_element_type=F32)
    x = x + jnp.dot(sw_ref[...], wo_ref[w + DIFF_WIDTH:, :], preferred_element_type=F32)
    ms = jnp.mean(x * x, axis=-1, keepdims=True)
    hn = (x * lax.rsqrt(ms + EPS) * n2_ref[...]).astype(BF16)
    mlp = jnp.zeros_like(x)
    for c0 in range(0, D_FF, ff_chunk):
        h = jnp.dot(hn, wu_ref[:, c0:c0 + ff_chunk], preferred_element_type=F32)
        h = jnp.square(jnp.maximum(h, 0.0)).astype(BF16)
        mlp = mlp + jnp.dot(h, wd_ref[c0:c0 + ff_chunk, :], preferred_element_type=F32)
    x = x + mlp
    if final_norm:
        ms = jnp.mean(x * x, axis=-1, keepdims=True)
        x = x * lax.rsqrt(ms + EPS) * fn_ref[...]
    y_ref[...] = x


def _outmlp(x2d, o_f, o_b, z, df, sw, dnw_e, wo, n2, wu, wd, fn, final_norm, tm):
    t, d = x2d.shape
    row = lambda i: (i, 0)
    const = lambda i: (0, 0)
    resident = lambda shape: pl.BlockSpec(shape, const)
    return pl.pallas_call(
        functools.partial(_outmlp_kernel, final_norm=final_norm, ff_chunk=1024),
        grid=(t // tm,),
        in_specs=[
            pl.BlockSpec((tm, d), row),
            pl.BlockSpec((tm, DN_WIDTH), row), pl.BlockSpec((tm, DN_WIDTH), row),
            pl.BlockSpec((tm, DN_WIDTH), row), pl.BlockSpec((tm, DIFF_WIDTH), row),
            pl.BlockSpec((tm, SWA_WIDTH), row),
            pl.BlockSpec((1, DN_WIDTH), const),
            resident((d, d)), pl.BlockSpec((1, d), const),
            resident((d, D_FF)), resident((D_FF, d)),
            pl.BlockSpec((1, d), const),
        ],
        out_specs=pl.BlockSpec((tm, d), row),
        out_shape=jax.ShapeDtypeStruct((t, d), F32),
        compiler_params=pltpu.CompilerParams(
            dimension_semantics=("parallel",), vmem_limit_bytes=VMEM_LIMIT),
        name="outmlp",
    )(x2d, o_f, o_b, z, df, sw, dnw_e, wo, n2, wu, wd, fn)


def _pick(n, pref):
    t = min(n, pref)
    assert n % t == 0, (n, pref)
    return t


def _trunk(x, params):
    b, s, d = x.shape
    assert d == D_MODEL and s % DN_CHUNK == 0 and s % WINDOW == 0
    depth = params["w_in_r"].shape[0]
    tm = _pick(s, 512)
    cos_t, sin_t = _rope_tables(s)
    x2d = x.reshape(b * s, d)
    for li in range(depth):
        p = {k: v[li] for k, v in params.items() if k != "final_norm_w"}
        lam_init = 0.8 - 0.6 * math.exp(-0.3 * li)
        qkv, z, beta, g, dq, dk, dv, sq, sk, sv = _inproj(
            x2d, s, p["norm1_w"], p["w_in_r"], cos_t, sin_t, p["alog_e"], p["dtb_e"], tm)
        r3 = lambda a: a.reshape(b, s, a.shape[-1])
        o_f, o_b = _deltanet(r3(qkv), r3(beta), r3(g), p["conv_w"], _pick(s, 256))
        dvt = dv.reshape(b, s, DIFF_HEADS, 2 * DIFF_HEAD_DIM).transpose(0, 2, 3, 1)
        df = _diffattn(r3(dq), r3(dk), dvt, p["diff_lambda"], p["diff_norm_w"], lam_init,
                       _pick(s, 512), _pick(s, 2048), _pick(s, 512))
        sw = _swa(r3(sq), r3(sk), r3(sv), p["swa_sink"], _pick(s, 256))
        r2 = lambda a: a.reshape(b * s, a.shape[-1])
        x2d = _outmlp(x2d, r2(o_f), r2(o_b), z, r2(df), r2(sw), p["dn_norm_e"], p["w_out"],
                      p["norm2_w"], p["w_up"], p["w_down"], params["final_norm_w"],
                      li == depth - 1, tm)
    return x2d.reshape(b, s, d)


def kernel(x_prompt, x_sample, norm1_w, w_in, dn_conv_w, dn_a_log, dn_dt_bias, dn_norm_w,
           diff_lambda, diff_norm_w, swa_sink, w_out, norm2_w, w_up, w_down, final_norm_w):
    depth = w_in.shape[0]
    rep = lambda a: jnp.repeat(a.astype(F32).reshape(depth, 1, 2 * DN_HEADS), DN_HEAD_DIM, axis=2)
    params = {
        "norm1_w": norm1_w.astype(F32)[:, None, :],
        "w_in_r": jnp.take(w_in, jnp.asarray(_in_col_index()), axis=2).astype(BF16),
        "conv_w": jnp.pad(dn_conv_w.astype(F32), ((0, 0), (0, 8 - DN_CONV), (0, 0))),
        "alog_e": rep(dn_a_log),
        "dtb_e": rep(dn_dt_bias),
        "dn_norm_e": jnp.tile(dn_norm_w.astype(F32), (1, DN_HEADS))[:, None, :],
        "diff_lambda": diff_lambda.astype(F32),
        "diff_norm_w": diff_norm_w.astype(F32)[:, :, None],
        "swa_sink": swa_sink.astype(F32),
        "w_out": w_out.astype(BF16),
        "norm2_w": norm2_w.astype(F32)[:, None, :],
        "w_up": w_up.astype(BF16),
        "w_down": w_down.astype(BF16),
        "final_norm_w": final_norm_w.astype(F32)[None, :],
    }
    return _trunk(x_prompt, params), _trunk(x_sample, params)
```

```python
import functools
import math

import numpy as np
import jax
import jax.numpy as jnp
from jax import lax
from jax.experimental import pallas as pl
from jax.experimental.pallas import tpu as pltpu

F32 = jnp.float32
BF16 = jnp.bfloat16

D_MODEL = 1024
DN_HEADS = 4
DN_HEAD_DIM = 64
DN_WIDTH = DN_HEADS * DN_HEAD_DIM
DN_CONV = 5
DN_CHUNK = 64
DIFF_HEADS = 4
DIFF_HEAD_DIM = 64
DIFF_WIDTH = DIFF_HEADS * 2 * DIFF_HEAD_DIM
SWA_HEADS = 4
SWA_KV_HEADS = 2
SWA_HEAD_DIM = 64
SWA_WIDTH = SWA_HEADS * SWA_HEAD_DIM
WINDOW = 128
ROPE_THETA = 10000.0
D_FF = 4 * D_MODEL
EPS = 1e-6

LANES = 128
HALO = 16
ONES_ROWS = 16
LOG2E = math.log2(math.e)
VMEM_LIMIT = 56 * 1024 * 1024

_SEC_SIZES = (3 * DN_WIDTH, DN_WIDTH, 2 * DN_WIDTH, 2 * DN_WIDTH,
              DIFF_WIDTH, DIFF_WIDTH, DIFF_WIDTH, SWA_WIDTH, SWA_WIDTH, SWA_WIDTH)
_SEC_OFFS = tuple(int(v) for v in np.cumsum((0,) + _SEC_SIZES))
IN_COLS_R = _SEC_OFFS[-1]


def _in_col_index():
    o_z = 3 * DN_WIDTH
    o_b = o_z + DN_WIDTH
    o_a = o_b + 2 * DN_HEADS
    o_dq = o_a + 2 * DN_HEADS
    o_dk = o_dq + DIFF_WIDTH
    o_dv = o_dk + DIFF_WIDTH
    o_sq = o_dv + DIFF_WIDTH
    o_sk = o_sq + SWA_WIDTH
    o_sv = o_sk + SWA_KV_HEADS * SWA_HEAD_DIM
    grp = SWA_HEADS // SWA_KV_HEADS
    kv_rep = np.concatenate([np.tile(np.arange(SWA_HEAD_DIM) + g * SWA_HEAD_DIM, grp)
                             for g in range(SWA_KV_HEADS)])
    idx = np.concatenate([
        np.arange(0, o_z), np.arange(o_z, o_b),
        np.repeat(np.arange(o_b, o_a), DN_HEAD_DIM),
        np.repeat(np.arange(o_a, o_dq), DN_HEAD_DIM),
        np.arange(o_dq, o_dk), np.arange(o_dk, o_dv), np.arange(o_dv, o_sq),
        np.arange(o_sq, o_sk), o_sk + kv_rep, o_sv + kv_rep])
    assert idx.shape[0] == IN_COLS_R
    return idx.astype(np.int32)


def _rope_tables(seq):
    half = DIFF_HEAD_DIM // 2
    inv = ROPE_THETA ** (-jnp.arange(half, dtype=F32) / half)
    ang = jnp.arange(seq, dtype=F32)[:, None] * inv[None, :]
    cos, sin = jnp.cos(ang), jnp.sin(ang)
    reps = LANES // (2 * half)
    cos_t = jnp.tile(jnp.concatenate([cos, cos], axis=1), (1, reps))
    sin_t = jnp.tile(jnp.concatenate([-sin, sin], axis=1), (1, reps))
    return cos_t, sin_t


def _sigmoid(x):
    return 1.0 / (1.0 + jnp.exp(-x))


def _silu(x):
    return x * _sigmoid(x)


def _inproj_kernel(x_ref, nw_ref, w_ref, cos_ref, sin_ref, alog_ref, dtb_ref,
                   qkv_ref, z_ref, beta_ref, g_ref, dq_ref, dk_ref, dv_ref,
                   sq_ref, sk_ref, sv_ref):
    x = x_ref[...]
    ms = jnp.mean(x * x, axis=-1, keepdims=True)
    hn = (x * lax.rsqrt(ms + EPS) * nw_ref[...]).astype(BF16)
    cos = cos_ref[...]
    sin = sin_ref[...]
    lane = lax.broadcasted_iota(jnp.int32, cos.shape, 1)
    low = (lane % DIFF_HEAD_DIM) < (DIFF_HEAD_DIM // 2)

    def proj(sec):
        return jnp.dot(hn, w_ref[:, _SEC_OFFS[sec]:_SEC_OFFS[sec + 1]],
                       preferred_element_type=F32)

    def rope_store(sec, out_ref, scale):
        y_all = proj(sec)
        for c0 in range(0, _SEC_SIZES[sec], LANES):
            y = y_all[:, c0:c0 + LANES]
            up = pltpu.roll(y, DIFF_HEAD_DIM // 2, axis=1)
            down = pltpu.roll(y, LANES - DIFF_HEAD_DIM // 2, axis=1)
            r = y * cos + jnp.where(low, down, up) * sin
            if scale != 1.0:
                r = r * scale
            out_ref[:, c0:c0 + LANES] = r.astype(out_ref.dtype)

    qkv_ref[...] = proj(0).astype(qkv_ref.dtype)
    z_ref[...] = proj(1).astype(z_ref.dtype)
    beta_ref[...] = _sigmoid(proj(2))
    a = proj(3) + dtb_ref[...]
    softplus = jnp.maximum(a, 0.0) + jnp.log(1.0 + jnp.exp(-jnp.abs(a)))
    g_ref[...] = -jnp.exp(alog_ref[...]) * softplus
    rope_store(4, dq_ref, DIFF_HEAD_DIM ** -0.5 * LOG2E)
    rope_store(5, dk_ref, 1.0)
    dv_ref[...] = proj(6).astype(dv_ref.dtype)
    rope_store(7, sq_ref, SWA_HEAD_DIM ** -0.5)
    rope_store(8, sk_ref, 1.0)
    sv_ref[...] = proj(9).astype(sv_ref.dtype)


def _inproj(x2d, seq, nw, w_r, cos_t, sin_t, alog_e, dtb_e, tm):
    t, d = x2d.shape
    nseq = seq // tm
    row = lambda i: (i, 0)
    const = lambda i: (0, 0)
    widths = (3 * DN_WIDTH, DN_WIDTH, 2 * DN_WIDTH, 2 * DN_WIDTH, DIFF_WIDTH, DIFF_WIDTH,
              DIFF_WIDTH, SWA_WIDTH, SWA_WIDTH, SWA_WIDTH)
    dtypes = (BF16, BF16, F32, F32, BF16, BF16, BF16, BF16, BF16, BF16)
    return pl.pallas_call(
        _inproj_kernel,
        grid=(t // tm,),
        in_specs=[
            pl.BlockSpec((tm, d), row),
            pl.BlockSpec((1, d), const),
            pl.BlockSpec((d, IN_COLS_R), const, pipeline_mode=pl.Buffered(1)),
            pl.BlockSpec((tm, LANES), lambda i: (i % nseq, 0)),
            pl.BlockSpec((tm, LANES), lambda i: (i % nseq, 0)),
            pl.BlockSpec((1, 2 * DN_WIDTH), const),
            pl.BlockSpec((1, 2 * DN_WIDTH), const),
        ],
        out_specs=[pl.BlockSpec((tm, w), row) for w in widths],
        out_shape=[jax.ShapeDtypeStruct((t, w), dt) for w, dt in zip(widths, dtypes)],
        compiler_params=pltpu.CompilerParams(
            dimension_semantics=("parallel",), vmem_limit_bytes=VMEM_LIMIT),
        name="inproj",
    )(x2d, nw, w_r, cos_t, sin_t, alog_e, dtb_e)


def _block_diag(m, bd_mask):
    return jnp.where(bd_mask, jnp.concatenate([m] * DN_HEADS, axis=0), jnp.zeros((), m.dtype))


def _dn_block_prep(reverse, x_ref, xp_ref, xn_ref, b_ref, g_ref, cw_ref, has_prev, has_next,
                   tb, bd_mask):
    c = DN_CHUNK
    w = DN_WIDTH
    xm = x_ref[0].astype(F32)
    xp = jnp.where(has_prev, xp_ref[0].astype(F32), 0.0)
    xn = jnp.where(has_next, xn_ref[0].astype(F32), 0.0)
    ext = jnp.concatenate([xp, xm, xn], axis=0)
    n_ext = tb + 2 * HALO
    cw = cw_ref[...]
    y = jnp.zeros_like(xm)
    for tap in range(DN_CONV):
        off = tap - DN_CONV // 2
        if off == 0:
            sh = xm
        else:
            sh = pltpu.roll(ext, (-off) % n_ext, axis=0)[HALO:HALO + tb]
        y = y + sh * cw[tap:tap + 1, :]
    y = _silu(y)
    q, k, v = y[:, :w], y[:, w:2 * w], y[:, 2 * w:]

    ones_bd = bd_mask.astype(BF16)

    def seg_sum(t):
        hi = t.astype(BF16)
        lo = (t - hi.astype(F32)).astype(BF16)
        return (jnp.dot(hi, ones_bd, preferred_element_type=F32)
                + jnp.dot(lo, ones_bd, preferred_element_type=F32))

    q = q * lax.rsqrt(seg_sum(q * q) + EPS) * (DN_HEAD_DIM ** -0.5)
    k = k * lax.rsqrt(seg_sum(k * k) + EPS)

    g = g_ref[0]

    i_c = lax.broadcasted_iota(jnp.int32, (c, w), 0)
    j_c = lax.broadcasted_iota(jnp.int32, (c, w), 1) % DN_HEAD_DIM
    if reverse:
        tri, strict, row_tri = i_c <= j_c, i_c < j_c, i_c >= j_c
    else:
        tri, strict, row_tri = i_c >= j_c, i_c > j_c, i_c <= j_c
    lvl_masks = []
    for lvl in range(int(math.log2(c))):
        siblings = (i_c >> (lvl + 1)) == (j_c >> (lvl + 1))
        odd_i = ((i_c >> lvl) & 1) == 1
        odd_j = ((j_c >> lvl) & 1) == 1
        lvl_masks.append(siblings & ((~odd_i & odd_j) if reverse else (odd_i & ~odd_j)))

    r_b = lax.broadcasted_iota(jnp.int32, (tb, tb), 0)
    c_b = lax.broadcasted_iota(jnp.int32, (tb, tb), 1)
    same = (r_b // c) == (c_b // c)
    cum = (same & ((r_b <= c_b) if reverse else (r_b >= c_b))).astype(F32)
    gc_all = jnp.dot(cum, g, preferred_element_type=F32, precision=lax.Precision.HIGHEST)
    return dict(q=q, k=k, v=v, beta=b_ref[0], g=g, gc=gc_all, tri=tri, strict=strict,
                row_tri=row_tri, eye=(i_c == j_c).astype(F32), lvl_masks=lvl_masks)


def _deltanet_kernel(xf_ref, xfp_ref, xfn_ref, xb_ref, xbp_ref, xbn_ref,
                     bf_ref, gf_ref, bb_ref, gb_ref, cw_ref,
                     of_ref, ob_ref, sf_ref, sb_ref, *, tb):
    n = pl.program_id(1)
    nb = pl.num_programs(1)
    c = DN_CHUNK
    w = DN_WIDTH
    nch = tb // c

    @pl.when(n == 0)
    def _():
        sf_ref[...] = jnp.zeros_like(sf_ref)
        sb_ref[...] = jnp.zeros_like(sb_ref)

    r_i = lax.broadcasted_iota(jnp.int32, (w, w), 0)
    l_i = lax.broadcasted_iota(jnp.int32, (w, w), 1)
    bd_mask = (r_i // DN_HEAD_DIM) == (l_i // DN_HEAD_DIM)
    bd2_mask = jnp.concatenate([bd_mask, bd_mask], axis=1)

    def bd2(a, b):
        return jnp.concatenate([_block_diag(a, bd_mask), _block_diag(b, bd_mask)], axis=1)

    prep = (_dn_block_prep(False, xf_ref, xfp_ref, xfn_ref, bf_ref, gf_ref, cw_ref,
                           n > 0, n < nb - 1, tb, bd_mask),
            _dn_block_prep(True, xb_ref, xbp_ref, xbn_ref, bb_ref, gb_ref, cw_ref,
                           n < nb - 1, n > 0, tb, bd_mask))
    units = [(d, t if d == 0 else nch - 1 - t) for t in range(nch) for d in range(2)]

    st = []
    for d, ci in units:
        p = prep[d]
        rows = slice(ci * c, (ci + 1) * c)
        qc, kc, vc, bc, gc = p["q"][rows], p["k"][rows], p["v"][rows], p["beta"][rows], p["gc"][rows]
        gc_row = jnp.sum(jnp.where(p["row_tri"], p["g"][rows], 0.0), axis=0, keepdims=True)
        decay = jnp.where(p["tri"], jnp.exp(jnp.where(p["tri"], gc - gc_row, 0.0)), 0.0)
        g_end = gc[0:1] if d == 1 else gc[c - 1:c]
        eg = jnp.exp(gc)
        kb = kc.astype(BF16)
        gram = lax.dot_general(jnp.concatenate([qc.astype(BF16), kb], axis=0),
                               _block_diag(kb, bd_mask),
                               (((1,), (1,)), ((), ())), preferred_element_type=F32)
        lmat = jnp.where(p["strict"], gram[c:] * bc * decay, 0.0)
        st.append(dict(a_intra=(gram[:c] * decay).astype(BF16), lmat=lmat,
                       tinv=p["eye"] - jnp.where(p["lvl_masks"][0], lmat, 0.0),
                       rhs_u=(vc * bc).astype(BF16), rhs_w=(kc * bc * eg).astype(BF16),
                       q_dec=qc * eg, k_dec=(kc * jnp.exp(g_end - gc)).astype(BF16),
                       gamma=jnp.exp(g_end)))

    for lvl in range(1, int(math.log2(c))):
        m1 = []
        for (d, ci), u in zip(units, st):
            loff = jnp.where(prep[d]["lvl_masks"][lvl], u["lmat"], 0.0).astype(BF16)
            m1.append(jnp.dot(loff, _block_diag(u["tinv"].astype(BF16), bd_mask),
                              preferred_element_type=F32))
        for u, m in zip(st, m1):
            u["tinv"] = u["tinv"] - jnp.dot(u["tinv"].astype(BF16),
                                            _block_diag(m.astype(BF16), bd_mask),
                                            preferred_element_type=F32)

    for u in st:
        uw = jnp.dot(u["tinv"].astype(BF16), bd2(u["rhs_u"], u["rhs_w"]),
                     preferred_element_type=F32)
        u["uw"] = uw.astype(BF16)
    for u in st:
        uwb = u["uw"]
        kuw = lax.dot_general(u["k_dec"], uwb, (((0,), (0,)), ((), ())),
                              preferred_element_type=F32)
        kuw = jnp.where(bd2_mask, kuw, 0.0)
        auw = jnp.dot(u["a_intra"], bd2(uwb[:, :w], uwb[:, w:]),
                      preferred_element_type=F32)
        u["b_bd"] = kuw[:, :w]
        u["lhs"] = jnp.concatenate([kuw[:, w:].astype(BF16),
                                    (u["q_dec"] - auw[:, w:]).astype(BF16)], axis=0)
        u["e"] = auw[:, :w]

    s_refs = (sf_ref, sb_ref)
    o_refs = (of_ref, ob_ref)
    state = [sf_ref[...], sb_ref[...]]
    for (d, ci), u in zip(units, st):
        r = jnp.dot(u["lhs"], state[d].astype(BF16), preferred_element_type=F32)
        o_refs[d][0, ci * c:(ci + 1) * c, :] = r[w:] + u["e"]
        state[d] = state[d] * u["gamma"] - r[:w] + u["b_bd"]
    for d in range(2):
        s_refs[d][...] = state[d]


def _deltanet(qkv, beta, g, cw, tb):
    b, s, _ = qkv.shape
    nb = s // tb
    hpb = tb // HALO
    nh = s // HALO
    w3 = 3 * DN_WIDTH
    fwd = lambda bi, n: (bi, n, 0)
    bwd = lambda bi, n: (bi, nb - 1 - n, 0)
    bwd1 = lambda bi, n: (bi, nb - 1 - n, 1)
    fwd_prev = lambda bi, n: (bi, jnp.maximum(n * hpb - 1, 0), 0)
    fwd_next = lambda bi, n: (bi, jnp.minimum((n + 1) * hpb, nh - 1), 0)
    bwd_prev = lambda bi, n: (bi, jnp.maximum((nb - 1 - n) * hpb - 1, 0), 0)
    bwd_next = lambda bi, n: (bi, jnp.minimum((nb - n) * hpb, nh - 1), 0)
    gate = lambda im: pl.BlockSpec((1, tb, DN_WIDTH), im)
    return pl.pallas_call(
        functools.partial(_deltanet_kernel, tb=tb),
        grid=(b, nb),
        in_specs=[
            pl.BlockSpec((1, tb, w3), fwd), pl.BlockSpec((1, HALO, w3), fwd_prev),
            pl.BlockSpec((1, HALO, w3), fwd_next),
            pl.BlockSpec((1, tb, w3), bwd), pl.BlockSpec((1, HALO, w3), bwd_prev),
            pl.BlockSpec((1, HALO, w3), bwd_next),
            gate(fwd), gate(fwd), gate(bwd1), gate(bwd1),
            pl.BlockSpec((8, w3), lambda bi, n: (0, 0)),
        ],
        out_specs=[pl.BlockSpec((1, tb, DN_WIDTH), fwd), pl.BlockSpec((1, tb, DN_WIDTH), bwd)],
        out_shape=[jax.ShapeDtypeStruct((b, s, DN_WIDTH), F32)] * 2,
        scratch_shapes=[pltpu.VMEM((DN_WIDTH, DN_WIDTH), F32)] * 2,
        compiler_params=pltpu.CompilerParams(
            dimension_semantics=("parallel", "arbitrary"), vmem_limit_bytes=VMEM_LIMIT),
        name="deltanet",
    )(qkv, qkv, qkv, qkv, qkv, qkv, beta, g, beta, g, cw)


def _diffattn_kernel(q_ref, k_ref, vt_ref, lam_ref, nw_ref, o_ref,
                     q0_ref, q1_ref, m_ref, acc_ref, *, lam_init, tks):
    ki = pl.program_id(3)
    d = DIFF_HEAD_DIM
    dv = 2 * d

    @pl.when(ki == 0)
    def _():
        q = q_ref[0]
        lane = lax.broadcasted_iota(jnp.int32, q.shape, 1)
        zero = jnp.zeros((), q.dtype)
        q0_ref[...] = jnp.where(lane < d, q, zero)
        q1_ref[...] = jnp.where(lane >= d, q, zero)
        m_ref[...] = jnp.full_like(m_ref, -jnp.inf)
        acc_ref[...] = jnp.zeros_like(acc_ref)

    tk = k_ref.shape[1]
    ones = jnp.ones((ONES_ROWS, tks), BF16)
    q_refs = (q0_ref, q1_ref)

    def scores(sub, j):
        kb = k_ref[0, sub * tks:(sub + 1) * tks, :]
        return lax.dot_general(kb, q_refs[j][...], (((1,), (1,)), ((), ())),
                               preferred_element_type=F32)

    chains = [(sub, j) for sub in range(tk // tks) for j in range(2)]
    m = [m_ref[0], m_ref[1]]
    acc = [acc_ref[0], acc_ref[1]]
    s_next = scores(*chains[0])
    for idx, (sub, j) in enumerate(chains):
        s = s_next
        if idx + 1 < len(chains):
            s_next = scores(*chains[idx + 1])
        vt = jnp.concatenate([vt_ref[0, 0, :, sub * tks:(sub + 1) * tks], ones], axis=0)
        m_new = jnp.maximum(m[j], jnp.max(s, axis=0, keepdims=True))
        alpha = jnp.exp2(m[j] - m_new)
        p = jnp.exp2(s - m_new).astype(BF16)
        acc[j] = alpha * acc[j] + jnp.dot(vt, p, preferred_element_type=F32)
        m[j] = m_new
    for j in range(2):
        m_ref[j] = m[j]
        acc_ref[j] = acc[j]

    @pl.when(ki == pl.num_programs(3) - 1)
    def _():
        lv = lam_ref[...]
        lam = (jnp.exp(jnp.sum(lv[0:1] * lv[1:2])) - jnp.exp(jnp.sum(lv[2:3] * lv[3:4]))
               + lam_init)
        a0 = acc_ref[0]
        a1 = acc_ref[1]
        o = a0[:dv] / a0[dv:dv + 1] - lam * (a1[:dv] / a1[dv:dv + 1])
        ms = jnp.mean(o * o, axis=0, keepdims=True)
        o = o * lax.rsqrt(ms + EPS) * nw_ref[...] * (1.0 - lam_init)
        o_ref[0] = o.T.astype(o_ref.dtype)


def _diffattn(dq, dk, dvt, lam_p, nw_col, lam_init, tq, tk, tks):
    b, s, _ = dq.shape
    hw = 2 * DIFF_HEAD_DIM
    return pl.pallas_call(
        functools.partial(_diffattn_kernel, lam_init=lam_init, tks=tks),
        grid=(b, DIFF_HEADS, s // tq, s // tk),
        in_specs=[
            pl.BlockSpec((1, tq, hw), lambda bi, h, qi, ki: (bi, qi, h)),
            pl.BlockSpec((1, tk, hw), lambda bi, h, qi, ki: (bi, ki, h)),
            pl.BlockSpec((1, 1, hw, tk), lambda bi, h, qi, ki: (bi, h, 0, ki)),
            pl.BlockSpec((4, DIFF_HEAD_DIM), lambda bi, h, qi, ki: (0, 0)),
            pl.BlockSpec((hw, 1), lambda bi, h, qi, ki: (0, 0)),
        ],
        out_specs=pl.BlockSpec((1, tq, hw), lambda bi, h, qi, ki: (bi, qi, h)),
        out_shape=jax.ShapeDtypeStruct((b, s, DIFF_WIDTH), BF16),
        scratch_shapes=[pltpu.VMEM((tq, hw), BF16), pltpu.VMEM((tq, hw), BF16),
                        pltpu.VMEM((2, 1, tq), F32),
                        pltpu.VMEM((2, hw + ONES_ROWS, tq), F32)],
        compiler_params=pltpu.CompilerParams(
            dimension_semantics=("parallel", "parallel", "parallel", "arbitrary"),
            vmem_limit_bytes=VMEM_LIMIT),
        name="diffattn",
    )(dq, dk, dvt, lam_p, nw_col)


def _swa_kernel(q_ref, kp_ref, km_ref, kn_ref, vp_ref, vm_ref, vn_ref, sink_ref, o_ref, *, tq):
    qi = pl.program_id(1)
    g = pl.program_id(2)
    nq = pl.num_programs(1)
    d = SWA_HEAD_DIM
    grp = SWA_HEADS // SWA_KV_HEADS
    q = q_ref[0]
    k = jnp.concatenate([kp_ref[0], km_ref[0], kn_ref[0]], axis=0)
    v = jnp.concatenate([vp_ref[0], vm_ref[0], vn_ref[0]], axis=0)
    nk = tq + 2 * WINDOW
    qpos = lax.broadcasted_iota(jnp.int32, (tq, nk), 0)
    koff = lax.broadcasted_iota(jnp.int32, (tq, nk), 1) - WINDOW
    valid = jnp.abs(koff - qpos) <= WINDOW
    valid &= (koff >= 0) | (qi > 0)
    valid &= (koff < tq) | (qi < nq - 1)
    lane = lax.broadcasted_iota(jnp.int32, q.shape, 1)
    zero = jnp.zeros((), q.dtype)
    out = jnp.zeros((tq, grp * d), F32)
    for j in range(grp):
        in_head = (lane >= j * d) & (lane < (j + 1) * d)
        s = lax.dot_general(jnp.where(in_head, q, zero), k, (((1,), (1,)), ((), ())),
                            preferred_element_type=F32)
        s = jnp.where(valid, s, -jnp.inf)
        sink = sink_ref[g * grp + j]
        m = jnp.maximum(jnp.max(s, axis=-1, keepdims=True), sink)
        p = jnp.exp(s - m)
        denom = jnp.sum(p, axis=-1, keepdims=True) + jnp.exp(sink - m)
        pv = jnp.dot(p.astype(BF16), v, preferred_element_type=F32)
        out = jnp.where(in_head, pv / denom, out)
    o_ref[0] = out.astype(o_ref.dtype)


def _swa(sq, sk, sv, sink, tq):
    b, s, _ = sq.shape
    nq = s // tq
    wpb = tq // WINDOW
    nw = s // WINDOW
    gw = (SWA_HEADS // SWA_KV_HEADS) * SWA_HEAD_DIM
    main = lambda bi, qi, g: (bi, qi, g)
    prev = lambda bi, qi, g: (bi, jnp.maximum(qi * wpb - 1, 0), g)
    nxt = lambda bi, qi, g: (bi, jnp.minimum((qi + 1) * wpb, nw - 1), g)
    return pl.pallas_call(
        functools.partial(_swa_kernel, tq=tq),
        grid=(b, nq, SWA_KV_HEADS),
        in_specs=[
            pl.BlockSpec((1, tq, gw), main),
            pl.BlockSpec((1, WINDOW, gw), prev), pl.BlockSpec((1, tq, gw), main),
            pl.BlockSpec((1, WINDOW, gw), nxt),
            pl.BlockSpec((1, WINDOW, gw), prev), pl.BlockSpec((1, tq, gw), main),
            pl.BlockSpec((1, WINDOW, gw), nxt),
            pl.BlockSpec(memory_space=pltpu.SMEM),
        ],
        out_specs=pl.BlockSpec((1, tq, gw), main),
        out_shape=jax.ShapeDtypeStruct((b, s, SWA_WIDTH), BF16),
        compiler_params=pltpu.CompilerParams(
            dimension_semantics=("parallel", "parallel", "parallel"),
            vmem_limit_bytes=VMEM_LIMIT),
        name="swa",
    )(sq, sk, sk, sk, sv, sv, sv, sink)


def _outmlp_kernel(x_ref, of_ref, ob_ref, z_ref, df_ref, sw_ref, dnw_ref, wo_ref,
                   n2_ref, wu_ref, wd_ref, fn_ref, y_ref, *, final_norm, ff_chunk):
    w = DN_WIDTH
    o = of_ref[...] + ob_ref[...]
    r_i = lax.broadcasted_iota(jnp.int32, (w, w), 0)
    l_i = lax.broadcasted_iota(jnp.int32, (w, w), 1)
    ones_bd = ((r_i // DN_HEAD_DIM) == (l_i // DN_HEAD_DIM)).astype(BF16)
    sq = o * o
    hi = sq.astype(BF16)
    lo = (sq - hi.astype(F32)).astype(BF16)
    ss = (jnp.dot(hi, ones_bd, preferred_element_type=F32)
          + jnp.dot(lo, ones_bd, preferred_element_type=F32))
    dn = o * lax.rsqrt(ss * (1.0 / DN_HEAD_DIM) + EPS) * dnw_ref[...]
    dn = dn * _silu(z_ref[...].astype(F32))
    x = x_ref[...]
    x = x + jnp.dot(dn.astype(BF16), wo_ref[0:w, :], preferred_element_type=F32)
    x = x + jnp.dot(df_ref[...], wo_ref[w:w + DIFF_WIDTH, :], preferred_element_type=F32)
    x = x + jnp.dot(sw_ref[...], wo_ref[w + DIFF_WIDTH:, :], preferred_element_type=F32)
    ms = jnp.mean(x * x, axis=-1, keepdims=True)
    hn = (x * lax.rsqrt(ms + EPS) * n2_ref[...]).astype(BF16)
    mlp = jnp.zeros_like(x)
    for c0 in range(0, D_FF, ff_chunk):
        h = jnp.dot(hn, wu_ref[:, c0:c0 + ff_chunk], preferred_element_type=F32)
        h = jnp.square(jnp.maximum(h, 0.0)).astype(BF16)
        mlp = mlp + jnp.dot(h, wd_ref[c0:c0 + ff_chunk, :], preferred_element_type=F32)
    x = x + mlp
    if final_norm:
        ms = jnp.mean(x * x, axis=-1, keepdims=True)
        x = x * lax.rsqrt(ms + EPS) * fn_ref[...]
    y_ref[...] = x


def _outmlp(x2d, o_f, o_b, z, df, sw, dnw_e, wo, n2, wu, wd, fn, final_norm, tm):
    t, d = x2d.shape
    row = lambda i: (i, 0)
    const = lambda i: (0, 0)
    resident = lambda shape: pl.BlockSpec(shape, const)
    return pl.pallas_call(
        functools.partial(_outmlp_kernel, final_norm=final_norm, ff_chunk=1024),
        grid=(t // tm,),
        in_specs=[
            pl.BlockSpec((tm, d), row),
            pl.BlockSpec((tm, DN_WIDTH), row), pl.BlockSpec((tm, DN_WIDTH), row),
            pl.BlockSpec((tm, DN_WIDTH), row), pl.BlockSpec((tm, DIFF_WIDTH), row),
            pl.BlockSpec((tm, SWA_WIDTH), row),
            pl.BlockSpec((1, DN_WIDTH), const),
            resident((d, d)), pl.BlockSpec((1, d), const),
            resident((d, D_FF)), resident((D_FF, d)),
            pl.BlockSpec((1, d), const),
        ],
        out_specs=pl.BlockSpec((tm, d), row),
        out_shape=jax.ShapeDtypeStruct((t, d), F32),
        compiler_params=pltpu.CompilerParams(
            dimension_semantics=("parallel",), vmem_limit_bytes=VMEM_LIMIT),
        name="outmlp",
    )(x2d, o_f, o_b, z, df, sw, dnw_e, wo, n2, wu, wd, fn)


def _pick(n, pref):
    t = min(n, pref)
    assert n % t == 0, (n, pref)
    return t


def _trunk(x, params):
    b, s, d = x.shape
    assert d == D_MODEL and s % DN_CHUNK == 0 and s % WINDOW == 0
    depth = params["w_in_r"].shape[0]
    tm = _pick(s, 512)
    cos_t, sin_t = _rope_tables(s)
    x2d = x.reshape(b * s, d)
    for li in range(depth):
        p = {k: v[li] for k, v in params.items() if k != "final_norm_w"}
        lam_init = 0.8 - 0.6 * math.exp(-0.3 * li)
        qkv, z, beta, g, dq, dk, dv, sq, sk, sv = _inproj(
            x2d, s, p["norm1_w"], p["w_in_r"], cos_t, sin_t, p["alog_e"], p["dtb_e"], tm)
        r3 = lambda a: a.reshape(b, s, a.shape[-1])
        o_f, o_b = _deltanet(r3(qkv), r3(beta), r3(g), p["conv_w"], _pick(s, 256))
        dvt = dv.reshape(b, s, DIFF_HEADS, 2 * DIFF_HEAD_DIM).transpose(0, 2, 3, 1)
        df = _diffattn(r3(dq), r3(dk), dvt, p["diff_lambda"], p["diff_norm_w"], lam_init,
                       _pick(s, 512), _pick(s, 2048), _pick(s, 512))
        sw = _swa(r3(sq), r3(sk), r3(sv), p["swa_sink"], _pick(s, 256))
        r2 = lambda a: a.reshape(b * s, a.shape[-1])
        x2d = _outmlp(x2d, r2(o_f), r2(o_b), z, r2(df), r2(sw), p["dn_norm_e"], p["w_out"],
                      p["norm2_w"], p["w_up"], p["w_down"], params["final_norm_w"],
                      li == depth - 1, tm)
    return x2d.reshape(b, s, d)


def kernel(x_prompt, x_sample, norm1_w, w_in, dn_conv_w, dn_a_log, dn_dt_bias, dn_norm_w,
           diff_lambda, diff_norm_w, swa_sink, w_out, norm2_w, w_up, w_down, final_norm_w):
    depth = w_in.shape[0]
    rep = lambda a: jnp.repeat(a.astype(F32).reshape(depth, 1, 2 * DN_HEADS), DN_HEAD_DIM, axis=2)
    params = {
        "norm1_w": norm1_w.astype(F32)[:, None, :],
        "w_in_r": jnp.take(w_in, jnp.asarray(_in_col_index()), axis=2).astype(BF16),
        "conv_w": jnp.pad(dn_conv_w.astype(F32), ((0, 0), (0, 8 - DN_CONV), (0, 0))),
        "alog_e": rep(dn_a_log),
        "dtb_e": rep(dn_dt_bias),
        "dn_norm_e": jnp.tile(dn_norm_w.astype(F32), (1, DN_HEADS))[:, None, :],
        "diff_lambda": diff_lambda.astype(F32),
        "diff_norm_w": diff_norm_w.astype(F32)[:, :, None],
        "swa_sink": swa_sink.astype(F32),
        "w_out": w_out.astype(BF16),
        "norm2_w": norm2_w.astype(F32)[:, None, :],
        "w_up": w_up.astype(BF16),
        "w_down": w_down.astype(BF16),
        "final_norm_w": final_norm_w.astype(F32)[None, :],
    }
    return _trunk(x_prompt, params), _trunk(x_sample, params)
```

```python
import functools
import math

import numpy as np
import jax
import jax.numpy as jnp
from jax import lax
from jax.experimental import pallas as pl
from jax.experimental.pallas import tpu as pltpu

F32 = jnp.float32
BF16 = jnp.bfloat16

D_MODEL = 1024
DN_HEADS = 4
DN_HEAD_DIM = 64
DN_WIDTH = DN_HEADS * DN_HEAD_DIM
DN_CONV = 5
DN_CHUNK = 64
DIFF_HEADS = 4
DIFF_HEAD_DIM = 64
DIFF_WIDTH = DIFF_HEADS * 2 * DIFF_HEAD_DIM
SWA_HEADS = 4
SWA_KV_HEADS = 2
SWA_HEAD_DIM = 64
SWA_WIDTH = SWA_HEADS * SWA_HEAD_DIM
WINDOW = 128
ROPE_THETA = 10000.0
D_FF = 4 * D_MODEL
EPS = 1e-6

LANES = 128
HALO = 16
ONES_ROWS = 16
LOG2E = math.log2(math.e)
VMEM_LIMIT = 56 * 1024 * 1024

_SEC_SIZES = (3 * DN_WIDTH, DN_WIDTH, 2 * DN_WIDTH, 2 * DN_WIDTH,
              DIFF_WIDTH, DIFF_WIDTH, DIFF_WIDTH, SWA_WIDTH, SWA_WIDTH, SWA_WIDTH)
_SEC_OFFS = tuple(int(v) for v in np.cumsum((0,) + _SEC_SIZES))
IN_COLS_R = _SEC_OFFS[-1]


def _in_col_index():
    o_z = 3 * DN_WIDTH
    o_b = o_z + DN_WIDTH
    o_a = o_b + 2 * DN_HEADS
    o_dq = o_a + 2 * DN_HEADS
    o_dk = o_dq + DIFF_WIDTH
    o_dv = o_dk + DIFF_WIDTH
    o_sq = o_dv + DIFF_WIDTH
    o_sk = o_sq + SWA_WIDTH
    o_sv = o_sk + SWA_KV_HEADS * SWA_HEAD_DIM
    grp = SWA_HEADS // SWA_KV_HEADS
    kv_rep = np.concatenate([np.tile(np.arange(SWA_HEAD_DIM) + g * SWA_HEAD_DIM, grp)
                             for g in range(SWA_KV_HEADS)])
    idx = np.concatenate([
        np.arange(0, o_z), np.arange(o_z, o_b),
        np.repeat(np.arange(o_b, o_a), DN_HEAD_DIM),
        np.repeat(np.arange(o_a, o_dq), DN_HEAD_DIM),
        np.arange(o_dq, o_dk), np.arange(o_dk, o_dv), np.arange(o_dv, o_sq),
        np.arange(o_sq, o_sk), o_sk + kv_rep, o_sv + kv_rep])
    assert idx.shape[0] == IN_COLS_R
    return idx.astype(np.int32)


def _rope_tables(seq):
    half = DIFF_HEAD_DIM // 2
    inv = ROPE_THETA ** (-jnp.arange(half, dtype=F32) / half)
    ang = jnp.arange(seq, dtype=F32)[:, None] * inv[None, :]
    cos, sin = jnp.cos(ang), jnp.sin(ang)
    reps = LANES // (2 * half)
    cos_t = jnp.tile(jnp.concatenate([cos, cos], axis=1), (1, reps))
    sin_t = jnp.tile(jnp.concatenate([-sin, sin], axis=1), (1, reps))
    return cos_t, sin_t


def _sigmoid(x):
    return 1.0 / (1.0 + jnp.exp(-x))


def _silu(x):
    return x * _sigmoid(x)


def _inproj_kernel(x_ref, nw_ref, w_ref, cos_ref, sin_ref, alog_ref, dtb_ref,
                   qkv_ref, z_ref, beta_ref, g_ref, dq_ref, dk_ref, dv_ref,
                   sq_ref, sk_ref, sv_ref):
    x = x_ref[...]
    ms = jnp.mean(x * x, axis=-1, keepdims=True)
    hn = (x * lax.rsqrt(ms + EPS) * nw_ref[...]).astype(BF16)
    cos = cos_ref[...]
    sin = sin_ref[...]
    lane = lax.broadcasted_iota(jnp.int32, cos.shape, 1)
    low = (lane % DIFF_HEAD_DIM) < (DIFF_HEAD_DIM // 2)

    def proj(sec):
        return jnp.dot(hn, w_ref[:, _SEC_OFFS[sec]:_SEC_OFFS[sec + 1]],
                       preferred_element_type=F32)

    def rope_store(sec, out_ref, scale):
        y_all = proj(sec)
        for c0 in range(0, _SEC_SIZES[sec], LANES):
            y = y_all[:, c0:c0 + LANES]
            up = pltpu.roll(y, DIFF_HEAD_DIM // 2, axis=1)
            down = pltpu.roll(y, LANES - DIFF_HEAD_DIM // 2, axis=1)
            r = y * cos + jnp.where(low, down, up) * sin
            if scale != 1.0:
                r = r * scale
            out_ref[:, c0:c0 + LANES] = r.astype(out_ref.dtype)

    qkv_ref[...] = proj(0).astype(qkv_ref.dtype)
    z_ref[...] = proj(1).astype(z_ref.dtype)
    beta_ref[...] = _sigmoid(proj(2))
    a = proj(3) + dtb_ref[...]
    softplus = jnp.maximum(a, 0.0) + jnp.log(1.0 + jnp.exp(-jnp.abs(a)))
    g_ref[...] = -jnp.exp(alog_ref[...]) * softplus
    rope_store(4, dq_ref, DIFF_HEAD_DIM ** -0.5 * LOG2E)
    rope_store(5, dk_ref, 1.0)
    dv_ref[...] = proj(6).astype(dv_ref.dtype)
    rope_store(7, sq_ref, SWA_HEAD_DIM ** -0.5)
    rope_store(8, sk_ref, 1.0)
    sv_ref[...] = proj(9).astype(sv_ref.dtype)


def _inproj(x2d, seq, nw, w_r, cos_t, sin_t, alog_e, dtb_e, tm):
    t, d = x2d.shape
    nseq = seq // tm
    row = lambda i: (i, 0)
    const = lambda i: (0, 0)
    widths = (3 * DN_WIDTH, DN_WIDTH, 2 * DN_WIDTH, 2 * DN_WIDTH, DIFF_WIDTH, DIFF_WIDTH,
              DIFF_WIDTH, SWA_WIDTH, SWA_WIDTH, SWA_WIDTH)
    dtypes = (BF16, BF16, F32, F32, BF16, BF16, BF16, BF16, BF16, BF16)
    return pl.pallas_call(
        _inproj_kernel,
        grid=(t // tm,),
        in_specs=[
            pl.BlockSpec((tm, d), row),
            pl.BlockSpec((1, d), const),
            pl.BlockSpec((d, IN_COLS_R), const, pipeline_mode=pl.Buffered(1)),
            pl.BlockSpec((tm, LANES), lambda i: (i % nseq, 0)),
            pl.BlockSpec((tm, LANES), lambda i: (i % nseq, 0)),
            pl.BlockSpec((1, 2 * DN_WIDTH), const),
            pl.BlockSpec((1, 2 * DN_WIDTH), const),
        ],
        out_specs=[pl.BlockSpec((tm, w), row) for w in widths],
        out_shape=[jax.ShapeDtypeStruct((t, w), dt) for w, dt in zip(widths, dtypes)],
        compiler_params=pltpu.CompilerParams(
            dimension_semantics=("parallel",), vmem_limit_bytes=VMEM_LIMIT),
        name="inproj",
    )(x2d, nw, w_r, cos_t, sin_t, alog_e, dtb_e)


def _block_diag(m, bd_mask):
    return jnp.where(bd_mask, jnp.concatenate([m] * DN_HEADS, axis=0), jnp.zeros((), m.dtype))


def _dn_block_prep(reverse, x_ref, xp_ref, xn_ref, b_ref, g_ref, cw_ref, has_prev, has_next,
                   tb, bd_mask):
    c = DN_CHUNK
    w = DN_WIDTH
    xm = x_ref[0].astype(F32)
    xp = jnp.where(has_prev, xp_ref[0].astype(F32), 0.0)
    xn = jnp.where(has_next, xn_ref[0].astype(F32), 0.0)
    ext = jnp.concatenate([xp, xm, xn], axis=0)
    n_ext = tb + 2 * HALO
    cw = cw_ref[...]
    y = jnp.zeros_like(xm)
    for tap in range(DN_CONV):
        off = tap - DN_CONV // 2
        if off == 0:
            sh = xm
        else:
            sh = pltpu.roll(ext, (-off) % n_ext, axis=0)[HALO:HALO + tb]
        y = y + sh * cw[tap:tap + 1, :]
    y = _silu(y)
    q, k, v = y[:, :w], y[:, w:2 * w], y[:, 2 * w:]

    ones_bd = bd_mask.astype(BF16)

    def seg_sum(t):
        hi = t.astype(BF16)
        lo = (t - hi.astype(F32)).astype(BF16)
        return (jnp.dot(hi, ones_bd, preferred_element_type=F32)
                + jnp.dot(lo, ones_bd, preferred_element_type=F32))

    q = q * lax.rsqrt(seg_sum(q * q) + EPS) * (DN_HEAD_DIM ** -0.5)
    k = k * lax.rsqrt(seg_sum(k * k) + EPS)

    g = g_ref[0]

    i_c = lax.broadcasted_iota(jnp.int32, (c, w), 0)
    j_c = lax.broadcasted_iota(jnp.int32, (c, w), 1) % DN_HEAD_DIM
    if reverse:
        tri, strict, row_tri = i_c <= j_c, i_c < j_c, i_c >= j_c
    else:
        tri, strict, row_tri = i_c >= j_c, i_c > j_c, i_c <= j_c
    lvl_masks = []
    for lvl in range(int(math.log2(c))):
        siblings = (i_c >> (lvl + 1)) == (j_c >> (lvl + 1))
        odd_i = ((i_c >> lvl) & 1) == 1
        odd_j = ((j_c >> lvl) & 1) == 1
        lvl_masks.append(siblings & ((~odd_i & odd_j) if reverse else (odd_i & ~odd_j)))

    r_b = lax.broadcasted_iota(jnp.int32, (tb, tb), 0)
    c_b = lax.broadcasted_iota(jnp.int32, (tb, tb), 1)
    same = (r_b // c) == (c_b // c)
    cum = (same & ((r_b <= c_b) if reverse else (r_b >= c_b))).astype(F32)
    gc_all = jnp.dot(cum, g, preferred_element_type=F32, precision=lax.Precision.HIGHEST)
    return dict(q=q, k=k, v=v, beta=b_ref[0], g=g, gc=gc_all, tri=tri, strict=strict,
                row_tri=row_tri, eye=(i_c == j_c).astype(F32), lvl_masks=lvl_masks)


def _deltanet_kernel(xf_ref, xfp_ref, xfn_ref, xb_ref, xbp_ref, xbn_ref,
                     bf_ref, gf_ref, bb_ref, gb_ref, cw_ref,
                     of_ref, ob_ref, sf_ref, sb_ref, *, tb):
    n = pl.program_id(1)
    nb = pl.num_programs(1)
    c = DN_CHUNK
    w = DN_WIDTH
    nch = tb // c

    @pl.when(n == 0)
    def _():
        sf_ref[...] = jnp.zeros_like(sf_ref)
        sb_ref[...] = jnp.zeros_like(sb_ref)

    r_i = lax.broadcasted_iota(jnp.int32, (w, w), 0)
    l_i = lax.broadcasted_iota(jnp.int32, (w, w), 1)
    bd_mask = (r_i // DN_HEAD_DIM) == (l_i // DN_HEAD_DIM)
    bd2_mask = jnp.concatenate([bd_mask, bd_mask], axis=1)

    def bd2(a, b):
        return jnp.concatenate([_block_diag(a, bd_mask), _block_diag(b, bd_mask)], axis=1)

    prep = (_dn_block_prep(False, xf_ref, xfp_ref, xfn_ref, bf_ref, gf_ref, cw_ref,
                           n > 0, n < nb - 1, tb, bd_mask),
            _dn_block_prep(True, xb_ref, xbp_ref, xbn_ref, bb_ref, gb_ref, cw_ref,
                           n < nb - 1, n > 0, tb, bd_mask))
    units = [(d, t if d == 0 else nch - 1 - t) for t in range(nch) for d in range(2)]

    st = []
    for d, ci in units:
        p = prep[d]
        rows = slice(ci * c, (ci + 1) * c)
        qc, kc, vc, bc, gc = p["q"][rows], p["k"][rows], p["v"][rows], p["beta"][rows], p["gc"][rows]
        gc_row = jnp.sum(jnp.where(p["row_tri"], p["g"][rows], 0.0), axis=0, keepdims=True)
        decay = jnp.where(p["tri"], jnp.exp(jnp.where(p["tri"], gc - gc_row, 0.0)), 0.0)
        g_end = gc[0:1] if d == 1 else gc[c - 1:c]
        eg = jnp.exp(gc)
        kb = kc.astype(BF16)
        gram = lax.dot_general(jnp.concatenate([qc.astype(BF16), kb], axis=0),
                               _block_diag(kb, bd_mask),
                               (((1,), (1,)), ((), ())), preferred_element_type=F32)
        lmat = jnp.where(p["strict"], gram[c:] * bc * decay, 0.0)
        st.append(dict(a_intra=(gram[:c] * decay).astype(BF16), lmat=lmat,
                       tinv=p["eye"] - jnp.where(p["lvl_masks"][0], lmat, 0.0),
                       rhs_u=(vc * bc).astype(BF16), rhs_w=(kc * bc * eg).astype(BF16),
                       q_dec=qc * eg, k_dec=(kc * jnp.exp(g_end - gc)).astype(BF16),
                       gamma=jnp.exp(g_end)))

    for lvl in range(1, int(math.log2(c))):
        m1 = []
        for (d, ci), u in zip(units, st):
            loff = jnp.where(prep[d]["lvl_masks"][lvl], u["lmat"], 0.0).astype(BF16)
            m1.append(jnp.dot(loff, _block_diag(u["tinv"].astype(BF16), bd_mask),
                              preferred_element_type=F32))
        for u, m in zip(st, m1):
            u["tinv"] = u["tinv"] - jnp.dot(u["tinv"].astype(BF16),
                                            _block_diag(m.astype(BF16), bd_mask),
                                            preferred_element_type=F32)

    for u in st:
        uw = jnp.dot(u["tinv"].astype(BF16), bd2(u["rhs_u"], u["rhs_w"]),
                     preferred_element_type=F32)
        u["uw"] = uw.astype(BF16)
    for u in st:
        uwb = u["uw"]
        kuw = lax.dot_general(u["k_dec"], uwb, (((0,), (0,)), ((), ())),
                              preferred_element_type=F32)
        kuw = jnp.where(bd2_mask, kuw, 0.0)
        auw = jnp.dot(u["a_intra"], bd2(uwb[:, :w], uwb[:, w:]),
                      preferred_element_type=F32)
        u["b_bd"] = kuw[:, :w]
        u["lhs"] = jnp.concatenate([kuw[:, w:].astype(BF16),
                                    (u["q_dec"] - auw[:, w:]).astype(BF16)], axis=0)
        u["e"] = auw[:, :w]

    s_refs = (sf_ref, sb_ref)
    o_refs = (of_ref, ob_ref)
    state = [sf_ref[...], sb_ref[...]]
    for (d, ci), u in zip(units, st):
        r = jnp.dot(u["lhs"], state[d].astype(BF16), preferred_element_type=F32)
        o_refs[d][0, ci * c:(ci + 1) * c, :] = r[w:] + u["e"]
        state[d] = state[d] * u["gamma"] - r[:w] + u["b_bd"]
    for d in range(2):
        s_refs[d][...] = state[d]


def _deltanet(qkv, beta, g, cw, tb):
    b, s, _ = qkv.shape
    nb = s // tb
    hpb = tb // HALO
    nh = s // HALO
    w3 = 3 * DN_WIDTH
    fwd = lambda bi, n: (bi, n, 0)
    bwd = lambda bi, n: (bi, nb - 1 - n, 0)
    bwd1 = lambda bi, n: (bi, nb - 1 - n, 1)
    fwd_prev = lambda bi, n: (bi, jnp.maximum(n * hpb - 1, 0), 0)
    fwd_next = lambda bi, n: (bi, jnp.minimum((n + 1) * hpb, nh - 1), 0)
    bwd_prev = lambda bi, n: (bi, jnp.maximum((nb - 1 - n) * hpb - 1, 0), 0)
    bwd_next = lambda bi, n: (bi, jnp.minimum((nb - n) * hpb, nh - 1), 0)
    gate = lambda im: pl.BlockSpec((1, tb, DN_WIDTH), im)
    return pl.pallas_call(
        functools.partial(_deltanet_kernel, tb=tb),
        grid=(b, nb),
        in_specs=[
            pl.BlockSpec((1, tb, w3), fwd), pl.BlockSpec((1, HALO, w3), fwd_prev),
            pl.BlockSpec((1, HALO, w3), fwd_next),
            pl.BlockSpec((1, tb, w3), bwd), pl.BlockSpec((1, HALO, w3), bwd_prev),
            pl.BlockSpec((1, HALO, w3), bwd_next),
            gate(fwd), gate(fwd), gate(bwd1), gate(bwd1),
            pl.BlockSpec((8, w3), lambda bi, n: (0, 0)),
        ],
        out_specs=[pl.BlockSpec((1, tb, DN_WIDTH), fwd), pl.BlockSpec((1, tb, DN_WIDTH), bwd)],
        out_shape=[jax.ShapeDtypeStruct((b, s, DN_WIDTH), F32)] * 2,
        scratch_shapes=[pltpu.VMEM((DN_WIDTH, DN_WIDTH), F32)] * 2,
        compiler_params=pltpu.CompilerParams(
            dimension_semantics=("parallel", "arbitrary"), vmem_limit_bytes=VMEM_LIMIT),
        name="deltanet",
    )(qkv, qkv, qkv, qkv, qkv, qkv, beta, g, beta, g, cw)


def _diffattn_kernel(q_ref, k_ref, vt_ref, lam_ref, nw_ref, o_ref,
                     q0_ref, q1_ref, m_ref, acc_ref, *, lam_init, tks):
    ki = pl.program_id(3)
    d = DIFF_HEAD_DIM
    dv = 2 * d

    @pl.when(ki == 0)
    def _():
        q = q_ref[0]
        lane = lax.broadcasted_iota(jnp.int32, q.shape, 1)
        zero = jnp.zeros((), q.dtype)
        q0_ref[...] = jnp.where(lane < d, q, zero)
        q1_ref[...] = jnp.where(lane >= d, q, zero)
        m_ref[...] = jnp.full_like(m_ref, -jnp.inf)
        acc_ref[...] = jnp.zeros_like(acc_ref)

    tk = k_ref.shape[1]
    ones = jnp.ones((ONES_ROWS, tks), BF16)
    q_refs = (q0_ref, q1_ref)

    def scores(sub, j):
        kb = k_ref[0, sub * tks:(sub + 1) * tks, :]
        return lax.dot_general(kb, q_refs[j][...], (((1,), (1,)), ((), ())),
                               preferred_element_type=F32)

    chains = [(sub, j) for sub in range(tk // tks) for j in range(2)]
    m = [m_ref[0], m_ref[1]]
    acc = [acc_ref[0], acc_ref[1]]
    s_next = scores(*chains[0])
    for idx, (sub, j) in enumerate(chains):
        s = s_next
        if idx + 1 < len(chains):
            s_next = scores(*chains[idx + 1])
        vt = jnp.concatenate([vt_ref[0, 0, :, sub * tks:(sub + 1) * tks], ones], axis=0)
        m_new = jnp.maximum(m[j], jnp.max(s, axis=0, keepdims=True))
        alpha = jnp.exp2(m[j] - m_new)
        p = jnp.exp2(s - m_new).astype(BF16)
        acc[j] = alpha * acc[j] + jnp.dot(vt, p, preferred_element_type=F32)
        m[j] = m_new
    for j in range(2):
        m_ref[j] = m[j]
        acc_ref[j] = acc[j]

    @pl.when(ki == pl.num_programs(3) - 1)
    def _():
        lv = lam_ref[...]
        lam = (jnp.exp(jnp.sum(lv[0:1] * lv[1:2])) - jnp.exp(jnp.sum(lv[2:3] * lv[3:4]))
               + lam_init)
        a0 = acc_ref[0]
        a1 = acc_ref[1]
        o = a0[:dv] / a0[dv:dv + 1] - lam * (a1[:dv] / a1[dv:dv + 1])
        ms = jnp.mean(o * o, axis=0, keepdims=True)
        o = o * lax.rsqrt(ms + EPS) * nw_ref[...] * (1.0 - lam_init)
        o_ref[0] = o.T.astype(o_ref.dtype)


def _diffattn(dq, dk, dvt, lam_p, nw_col, lam_init, tq, tk, tks):
    b, s, _ = dq.shape
    hw = 2 * DIFF_HEAD_DIM
    return pl.pallas_call(
        functools.partial(_diffattn_kernel, lam_init=lam_init, tks=tks),
        grid=(b, DIFF_HEADS, s // tq, s // tk),
        in_specs=[
            pl.BlockSpec((1, tq, hw), lambda bi, h, qi, ki: (bi, qi, h)),
            pl.BlockSpec((1, tk, hw), lambda bi, h, qi, ki: (bi, ki, h)),
            pl.BlockSpec((1, 1, hw, tk), lambda bi, h, qi, ki: (bi, h, 0, ki)),
            pl.BlockSpec((4, DIFF_HEAD_DIM), lambda bi, h, qi, ki: (0, 0)),
            pl.BlockSpec((hw, 1), lambda bi, h, qi, ki: (0, 0)),
        ],
        out_specs=pl.BlockSpec((1, tq, hw), lambda bi, h, qi, ki: (bi, qi, h)),
        out_shape=jax.ShapeDtypeStruct((b, s, DIFF_WIDTH), BF16),
        scratch_shapes=[pltpu.VMEM((tq, hw), BF16), pltpu.VMEM((tq, hw), BF16),
                        pltpu.VMEM((2, 1, tq), F32),
                        pltpu.VMEM((2, hw + ONES_ROWS, tq), F32)],
        compiler_params=pltpu.CompilerParams(
            dimension_semantics=("parallel", "parallel", "parallel", "arbitrary"),
            vmem_limit_bytes=VMEM_LIMIT),
        name="diffattn",
    )(dq, dk, dvt, lam_p, nw_col)


def _swa_kernel(q_ref, kp_ref, km_ref, kn_ref, vp_ref, vm_ref, vn_ref, sink_ref, o_ref, *, tq):
    qi = pl.program_id(1)
    g = pl.program_id(2)
    nq = pl.num_programs(1)
    d = SWA_HEAD_DIM
    grp = SWA_HEADS // SWA_KV_HEADS
    q = q_ref[0]
    k = jnp.concatenate([kp_ref[0], km_ref[0], kn_ref[0]], axis=0)
    v = jnp.concatenate([vp_ref[0], vm_ref[0], vn_ref[0]], axis=0)
    nk = tq + 2 * WINDOW
    qpos = lax.broadcasted_iota(jnp.int32, (tq, nk), 0)
    koff = lax.broadcasted_iota(jnp.int32, (tq, nk), 1) - WINDOW
    valid = jnp.abs(koff - qpos) <= WINDOW
    valid &= (koff >= 0) | (qi > 0)
    valid &= (koff < tq) | (qi < nq - 1)
    lane = lax.broadcasted_iota(jnp.int32, q.shape, 1)
    zero = jnp.zeros((), q.dtype)
    out = jnp.zeros((tq, grp * d), F32)
    for j in range(grp):
        in_head = (lane >= j * d) & (lane < (j + 1) * d)
        s = lax.dot_general(jnp.where(in_head, q, zero), k, (((1,), (1,)), ((), ())),
                            preferred_element_type=F32)
        s = jnp.where(valid, s, -jnp.inf)
        sink = sink_ref[g * grp + j]
        m = jnp.maximum(jnp.max(s, axis=-1, keepdims=True), sink)
        p = jnp.exp(s - m)
        denom = jnp.sum(p, axis=-1, keepdims=True) + jnp.exp(sink - m)
        pv = jnp.dot(p.astype(BF16), v, preferred_element_type=F32)
        out = jnp.where(in_head, pv / denom, out)
    o_ref[0] = out.astype(o_ref.dtype)


def _swa(sq, sk, sv, sink, tq):
    b, s, _ = sq.shape
    nq = s // tq
    wpb = tq // WINDOW
    nw = s // WINDOW
    gw = (SWA_HEADS // SWA_KV_HEADS) * SWA_HEAD_DIM
    main = lambda bi, qi, g: (bi, qi, g)
    prev = lambda bi, qi, g: (bi, jnp.maximum(qi * wpb - 1, 0), g)
    nxt = lambda bi, qi, g: (bi, jnp.minimum((qi + 1) * wpb, nw - 1), g)
    return pl.pallas_call(
        functools.partial(_swa_kernel, tq=tq),
        grid=(b, nq, SWA_KV_HEADS),
        in_specs=[
            pl.BlockSpec((1, tq, gw), main),
            pl.BlockSpec((1, WINDOW, gw), prev), pl.BlockSpec((1, tq, gw), main),
            pl.BlockSpec((1, WINDOW, gw), nxt),
            pl.BlockSpec((1, WINDOW, gw), prev), pl.BlockSpec((1, tq, gw), main),
            pl.BlockSpec((1, WINDOW, gw), nxt),
            pl.BlockSpec(memory_space=pltpu.SMEM),
        ],
        out_specs=pl.BlockSpec((1, tq, gw), main),
        out_shape=jax.ShapeDtypeStruct((b, s, SWA_WIDTH), BF16),
        compiler_params=pltpu.CompilerParams(
            dimension_semantics=("parallel", "parallel", "parallel"),
            vmem_limit_bytes=VMEM_LIMIT),
        name="swa",
    )(sq, sk, sk, sk, sv, sv, sv, sink)


def _outmlp_kernel(x_ref, of_ref, ob_ref, z_ref, df_ref, sw_ref, dnw_ref, wo_ref,
                   n2_ref, wu_ref, wd_ref, fn_ref, y_ref, *, final_norm, ff_chunk):
    w = DN_WIDTH
    o = of_ref[...] + ob_ref[...]
    r_i = lax.broadcasted_iota(jnp.int32, (w, w), 0)
    l_i = lax.broadcasted_iota(jnp.int32, (w, w), 1)
    ones_bd = ((r_i // DN_HEAD_DIM) == (l_i // DN_HEAD_DIM)).astype(BF16)
    sq = o * o
    hi = sq.astype(BF16)
    lo = (sq - hi.astype(F32)).astype(BF16)
    ss = (jnp.dot(hi, ones_bd, preferred_element_type=F32)
          + jnp.dot(lo, ones_bd, preferred_element_type=F32))
    dn = o * lax.rsqrt(ss * (1.0 / DN_HEAD_DIM) + EPS) * dnw_ref[...]
    dn = dn * _silu(z_ref[...].astype(F32))
    x = x_ref[...]
    x = x + jnp.dot(dn.astype(BF16), wo_ref[0:w, :], preferred_element_type=F32)
    x = x + jnp.dot(df_ref[...], wo_ref[w:w + DIFF_WIDTH, :], preferred_element_type=F32)
    x = x + jnp.dot(sw_ref[...], wo_ref[w + DIFF_WIDTH:, :], preferred_element_type=F32)
    ms = jnp.mean(x * x, axis=-1, keepdims=True)
    hn = (x * lax.rsqrt(ms + EPS) * n2_ref[...]).astype(BF16)
    mlp = jnp.zeros_like(x)
    for c0 in range(0, D_FF, ff_chunk):
        h = jnp.dot(hn, wu_ref[:, c0:c0 + ff_chunk], preferred_element_type=F32)
        h = jnp.square(jnp.maximum(h, 0.0)).astype(BF16)
        mlp = mlp + jnp.dot(h, wd_ref[c0:c0 + ff_chunk, :], preferred_element_type=F32)
    x = x + mlp
    if final_norm:
        ms = jnp.mean(x * x, axis=-1, keepdims=True)
        x = x * lax.rsqrt(ms + EPS) * fn_ref[...]
    y_ref[...] = x


def _outmlp(x2d, o_f, o_b, z, df, sw, dnw_e, wo, n2, wu, wd, fn, final_norm, tm):
    t, d = x2d.shape
    row = lambda i: (i, 0)
    const = lambda i: (0, 0)
    resident = lambda shape: pl.BlockSpec(shape, const)
    return pl.pallas_call(
        functools.partial(_outmlp_kernel, final_norm=final_norm, ff_chunk=1024),
        grid=(t // tm,),
        in_specs=[
            pl.BlockSpec((tm, d), row),
            pl.BlockSpec((tm, DN_WIDTH), row), pl.BlockSpec((tm, DN_WIDTH), row),
            pl.BlockSpec((tm, DN_WIDTH), row), pl.BlockSpec((tm, DIFF_WIDTH), row),
            pl.BlockSpec((tm, SWA_WIDTH), row),
            pl.BlockSpec((1, DN_WIDTH), const),
            resident((d, d)), pl.BlockSpec((1, d), const),
            resident((d, D_FF)), resident((D_FF, d)),
            pl.BlockSpec((1, d), const),
        ],
        out_specs=pl.BlockSpec((tm, d), row),
        out_shape=jax.ShapeDtypeStruct((t, d), F32),
        compiler_params=pltpu.CompilerParams(
            dimension_semantics=("parallel",), vmem_limit_bytes=VMEM_LIMIT),
        name="outmlp",
    )(x2d, o_f, o_b, z, df, sw, dnw_e, wo, n2, wu, wd, fn)


def _pick(n, pref):
    t = min(n, pref)
    assert n % t == 0, (n, pref)
    return t


def _trunk(x, params):
    b, s, d = x.shape
    assert d == D_MODEL and s % DN_CHUNK == 0 and s % WINDOW == 0
    depth = params["w_in_r"].shape[0]
    tm = _pick(s, 512)
    cos_t, sin_t = _rope_tables(s)
    x2d = x.reshape(b * s, d)
    for li in range(depth):
        p = {k: v[li] for k, v in params.items() if k != "final_norm_w"}
        lam_init = 0.8 - 0.6 * math.exp(-0.3 * li)
        qkv, z, beta, g, dq, dk, dv, sq, sk, sv = _inproj(
            x2d, s, p["norm1_w"], p["w_in_r"], cos_t, sin_t, p["alog_e"], p["dtb_e"], tm)
        r3 = lambda a: a.reshape(b, s, a.shape[-1])
        o_f, o_b = _deltanet(r3(qkv), r3(beta), r3(g), p["conv_w"], _pick(s, 256))
        dvt = dv.reshape(b, s, DIFF_HEADS, 2 * DIFF_HEAD_DIM).transpose(0, 2, 3, 1)
        df = _diffattn(r3(dq), r3(dk), dvt, p["diff_lambda"], p["diff_norm_w"], lam_init,
                       _pick(s, 512), _pick(s, 4096), _pick(s, 1024))
        sw = _swa(r3(sq), r3(sk), r3(sv), p["swa_sink"], _pick(s, 512))
        r2 = lambda a: a.reshape(b * s, a.shape[-1])
        x2d = _outmlp(x2d, r2(o_f), r2(o_b), z, r2(df), r2(sw), p["dn_norm_e"], p["w_out"],
                      p["norm2_w"], p["w_up"], p["w_down"], params["final_norm_w"],
                      li == depth - 1, tm)
    return x2d.reshape(b, s, d)


def kernel(x_prompt, x_sample, norm1_w, w_in, dn_conv_w, dn_a_log, dn_dt_bias, dn_norm_w,
           diff_lambda, diff_norm_w, swa_sink, w_out, norm2_w, w_up, w_down, final_norm_w):
    depth = w_in.shape[0]
    rep = lambda a: jnp.repeat(a.astype(F32).reshape(depth, 1, 2 * DN_HEADS), DN_HEAD_DIM, axis=2)
    params = {
        "norm1_w": norm1_w.astype(F32)[:, None, :],
        "w_in_r": jnp.take(w_in, jnp.asarray(_in_col_index()), axis=2).astype(BF16),
        "conv_w": jnp.pad(dn_conv_w.astype(F32), ((0, 0), (0, 8 - DN_CONV), (0, 0))),
        "alog_e": rep(dn_a_log),
        "dtb_e": rep(dn_dt_bias),
        "dn_norm_e": jnp.tile(dn_norm_w.astype(F32), (1, DN_HEADS))[:, None, :],
        "diff_lambda": diff_lambda.astype(F32),
        "diff_norm_w": diff_norm_w.astype(F32)[:, :, None],
        "swa_sink": swa_sink.astype(F32),
        "w_out": w_out.astype(BF16),
        "norm2_w": norm2_w.astype(F32)[:, None, :],
        "w_up": w_up.astype(BF16),
        "w_down": w_down.astype(BF16),
        "final_norm_w": final_norm_w.astype(F32)[None, :],
    }
    return _trunk(x_prompt, params), _trunk(x_sample, params)
```

```python
import functools
import math

import numpy as np
import jax
import jax.numpy as jnp
from jax import lax
from jax.experimental import pallas as pl
from jax.experimental.pallas import tpu as pltpu

F32 = jnp.float32
BF16 = jnp.bfloat16

D_MODEL = 1024
DN_HEADS = 4
DN_HEAD_DIM = 64
DN_WIDTH = DN_HEADS * DN_HEAD_DIM
DN_CONV = 5
DN_CHUNK = 64
DIFF_HEADS = 4
DIFF_HEAD_DIM = 64
DIFF_WIDTH = DIFF_HEADS * 2 * DIFF_HEAD_DIM
SWA_HEADS = 4
SWA_KV_HEADS = 2
SWA_HEAD_DIM = 64
SWA_WIDTH = SWA_HEADS * SWA_HEAD_DIM
WINDOW = 128
ROPE_THETA = 10000.0
D_FF = 4 * D_MODEL
EPS = 1e-6

LANES = 128
HALO = 16
ONES_ROWS = 16
LOG2E = math.log2(math.e)
VMEM_LIMIT = 56 * 1024 * 1024

_SEC_SIZES = (3 * DN_WIDTH, DN_WIDTH, 2 * DN_WIDTH, 2 * DN_WIDTH,
              DIFF_WIDTH, DIFF_WIDTH, DIFF_WIDTH, SWA_WIDTH, SWA_WIDTH, SWA_WIDTH)
_SEC_OFFS = tuple(int(v) for v in np.cumsum((0,) + _SEC_SIZES))
IN_COLS_R = _SEC_OFFS[-1]


def _reorder_w_in(w_in):
    depth, d, _ = w_in.shape
    w = w_in.astype(BF16)
    o_b = 4 * DN_WIDTH
    o_dq = o_b + 4 * DN_HEADS
    o_sk = o_dq + 3 * DIFF_WIDTH + SWA_WIDTH
    kv_w = SWA_KV_HEADS * SWA_HEAD_DIM
    grp = SWA_HEADS // SWA_KV_HEADS

    def per_group(cols):
        c = cols.reshape(depth, d, SWA_KV_HEADS, 1, SWA_HEAD_DIM)
        return jnp.broadcast_to(c, (depth, d, SWA_KV_HEADS, grp, SWA_HEAD_DIM)).reshape(depth, d, -1)

    out = jnp.concatenate([
        w[:, :, :o_b],
        jnp.repeat(w[:, :, o_b:o_dq], DN_HEAD_DIM, axis=2),
        w[:, :, o_dq:o_sk],
        per_group(w[:, :, o_sk:o_sk + kv_w]),
        per_group(w[:, :, o_sk + kv_w:o_sk + 2 * kv_w])], axis=2)
    assert out.shape[2] == IN_COLS_R
    return out


def _rope_tables(seq):
    half = DIFF_HEAD_DIM // 2
    inv = ROPE_THETA ** (-jnp.arange(half, dtype=F32) / half)
    ang = jnp.arange(seq, dtype=F32)[:, None] * inv[None, :]
    cos, sin = jnp.cos(ang), jnp.sin(ang)
    reps = LANES // (2 * half)
    cos_t = jnp.tile(jnp.concatenate([cos, cos], axis=1), (1, reps))
    sin_t = jnp.tile(jnp.concatenate([-sin, sin], axis=1), (1, reps))
    return cos_t, sin_t


def _sigmoid(x):
    return 1.0 / (1.0 + jnp.exp(-x))


def _silu(x):
    return x * _sigmoid(x)


def _inproj_kernel(x_ref, nw_ref, w_ref, cos_ref, sin_ref, alog_ref, dtb_ref,
                   qkv_ref, z_ref, beta_ref, g_ref, dq_ref, dk_ref, dv_ref,
                   sq_ref, sk_ref, sv_ref):
    x = x_ref[...]
    ms = jnp.mean(x * x, axis=-1, keepdims=True)
    hn = (x * lax.rsqrt(ms + EPS) * nw_ref[...]).astype(BF16)
    cos = cos_ref[...]
    sin = sin_ref[...]
    lane = lax.broadcasted_iota(jnp.int32, cos.shape, 1)
    low = (lane % DIFF_HEAD_DIM) < (DIFF_HEAD_DIM // 2)

    def proj(sec):
        return jnp.dot(hn, w_ref[:, _SEC_OFFS[sec]:_SEC_OFFS[sec + 1]],
                       preferred_element_type=F32)

    def rope_store(sec, out_ref, scale):
        y_all = proj(sec)
        for c0 in range(0, _SEC_SIZES[sec], LANES):
            y = y_all[:, c0:c0 + LANES]
            up = pltpu.roll(y, DIFF_HEAD_DIM // 2, axis=1)
            down = pltpu.roll(y, LANES - DIFF_HEAD_DIM // 2, axis=1)
            r = y * cos + jnp.where(low, down, up) * sin
            if scale != 1.0:
                r = r * scale
            out_ref[:, c0:c0 + LANES] = r.astype(out_ref.dtype)

    qkv_ref[...] = proj(0).astype(qkv_ref.dtype)
    z_ref[...] = proj(1).astype(z_ref.dtype)
    beta_ref[...] = _sigmoid(proj(2))
    a = proj(3) + dtb_ref[...]
    softplus = jnp.maximum(a, 0.0) + jnp.log(1.0 + jnp.exp(-jnp.abs(a)))
    g_ref[...] = -jnp.exp(alog_ref[...]) * softplus
    rope_store(4, dq_ref, DIFF_HEAD_DIM ** -0.5 * LOG2E)
    rope_store(5, dk_ref, 1.0)
    dv = proj(6)
    hw = 2 * DIFF_HEAD_DIM
    for h in range(DIFF_HEADS):
        dv_ref[0, h] = dv[:, h * hw:(h + 1) * hw].T.astype(dv_ref.dtype)
    rope_store(7, sq_ref, SWA_HEAD_DIM ** -0.5)
    rope_store(8, sk_ref, 1.0)
    sv_ref[...] = proj(9).astype(sv_ref.dtype)


def _inproj(x2d, seq, nw, w_r, cos_t, sin_t, alog_e, dtb_e, tm):
    t, d = x2d.shape
    nseq = seq // tm
    row = lambda i: (i, 0)
    const = lambda i: (0, 0)
    widths = (3 * DN_WIDTH, DN_WIDTH, 2 * DN_WIDTH, 2 * DN_WIDTH, DIFF_WIDTH, DIFF_WIDTH,
              DIFF_WIDTH, SWA_WIDTH, SWA_WIDTH, SWA_WIDTH)
    dtypes = (BF16, BF16, F32, F32, BF16, BF16, BF16, BF16, BF16, BF16)
    dvt_pos = 6
    hw = 2 * DIFF_HEAD_DIM
    dvt_spec = pl.BlockSpec((1, DIFF_HEADS, hw, tm), lambda i: (i // nseq, 0, 0, i % nseq))
    dvt_shape = jax.ShapeDtypeStruct((t // seq, DIFF_HEADS, hw, seq), BF16)
    return pl.pallas_call(
        _inproj_kernel,
        grid=(t // tm,),
        in_specs=[
            pl.BlockSpec((tm, d), row),
            pl.BlockSpec((1, d), const),
            pl.BlockSpec((d, IN_COLS_R), const, pipeline_mode=pl.Buffered(1)),
            pl.BlockSpec((tm, LANES), lambda i: (i % nseq, 0)),
            pl.BlockSpec((tm, LANES), lambda i: (i % nseq, 0)),
            pl.BlockSpec((1, 2 * DN_WIDTH), const),
            pl.BlockSpec((1, 2 * DN_WIDTH), const),
        ],
        out_specs=[dvt_spec if k == dvt_pos else pl.BlockSpec((tm, w), row)
                   for k, w in enumerate(widths)],
        out_shape=[dvt_shape if k == dvt_pos else jax.ShapeDtypeStruct((t, w), dt)
                   for k, (w, dt) in enumerate(zip(widths, dtypes))],
        compiler_params=pltpu.CompilerParams(
            dimension_semantics=("parallel",), vmem_limit_bytes=VMEM_LIMIT),
        name="inproj",
    )(x2d, nw, w_r, cos_t, sin_t, alog_e, dtb_e)


def _block_diag(m, bd_mask):
    return jnp.where(bd_mask, jnp.concatenate([m] * DN_HEADS, axis=0), jnp.zeros((), m.dtype))


def _dn_block_prep(reverse, x_ref, xp_ref, xn_ref, b_ref, g_ref, cw_ref, has_prev, has_next,
                   tb, bd_mask):
    c = DN_CHUNK
    w = DN_WIDTH
    xm = x_ref[0].astype(F32)
    xp = jnp.where(has_prev, xp_ref[0].astype(F32), 0.0)
    xn = jnp.where(has_next, xn_ref[0].astype(F32), 0.0)
    ext = jnp.concatenate([xp, xm, xn], axis=0)
    n_ext = tb + 2 * HALO
    cw = cw_ref[...]
    y = jnp.zeros_like(xm)
    for tap in range(DN_CONV):
        off = tap - DN_CONV // 2
        if off == 0:
            sh = xm
        else:
            sh = pltpu.roll(ext, (-off) % n_ext, axis=0)[HALO:HALO + tb]
        y = y + sh * cw[tap:tap + 1, :]
    y = _silu(y)
    q, k, v = y[:, :w], y[:, w:2 * w], y[:, 2 * w:]

    ones_bd = bd_mask.astype(BF16)

    def seg_sum(t):
        hi = t.astype(BF16)
        lo = (t - hi.astype(F32)).astype(BF16)
        return (jnp.dot(hi, ones_bd, preferred_element_type=F32)
                + jnp.dot(lo, ones_bd, preferred_element_type=F32))

    q = q * lax.rsqrt(seg_sum(q * q) + EPS) * (DN_HEAD_DIM ** -0.5)
    k = k * lax.rsqrt(seg_sum(k * k) + EPS)

    g = g_ref[0]

    i_c = lax.broadcasted_iota(jnp.int32, (c, w), 0)
    j_c = lax.broadcasted_iota(jnp.int32, (c, w), 1) % DN_HEAD_DIM
    if reverse:
        tri, strict, row_tri = i_c <= j_c, i_c < j_c, i_c >= j_c
    else:
        tri, strict, row_tri = i_c >= j_c, i_c > j_c, i_c <= j_c
    lvl_masks = []
    for lvl in range(int(math.log2(c))):
        siblings = (i_c >> (lvl + 1)) == (j_c >> (lvl + 1))
        odd_i = ((i_c >> lvl) & 1) == 1
        odd_j = ((j_c >> lvl) & 1) == 1
        lvl_masks.append(siblings & ((~odd_i & odd_j) if reverse else (odd_i & ~odd_j)))

    r_b = lax.broadcasted_iota(jnp.int32, (tb, tb), 0)
    c_b = lax.broadcasted_iota(jnp.int32, (tb, tb), 1)
    same = (r_b // c) == (c_b // c)
    cum = (same & ((r_b <= c_b) if reverse else (r_b >= c_b))).astype(F32)
    gc_all = jnp.dot(cum, g, preferred_element_type=F32, precision=lax.Precision.HIGHEST)
    return dict(q=q, k=k, v=v, beta=b_ref[0], g=g, gc=gc_all, tri=tri, strict=strict,
                row_tri=row_tri, eye=(i_c == j_c).astype(F32), lvl_masks=lvl_masks)


def _deltanet_kernel(xf_ref, xfp_ref, xfn_ref, xb_ref, xbp_ref, xbn_ref,
                     bf_ref, gf_ref, bb_ref, gb_ref, cw_ref,
                     of_ref, ob_ref, sf_ref, sb_ref, *, tb):
    n = pl.program_id(1)
    nb = pl.num_programs(1)
    c = DN_CHUNK
    w = DN_WIDTH
    nch = tb // c

    @pl.when(n == 0)
    def _():
        sf_ref[...] = jnp.zeros_like(sf_ref)
        sb_ref[...] = jnp.zeros_like(sb_ref)

    r_i = lax.broadcasted_iota(jnp.int32, (w, w), 0)
    l_i = lax.broadcasted_iota(jnp.int32, (w, w), 1)
    bd_mask = (r_i // DN_HEAD_DIM) == (l_i // DN_HEAD_DIM)
    bd2_mask = jnp.concatenate([bd_mask, bd_mask], axis=1)

    def bd2(a, b):
        return jnp.concatenate([_block_diag(a, bd_mask), _block_diag(b, bd_mask)], axis=1)

    prep = (_dn_block_prep(False, xf_ref, xfp_ref, xfn_ref, bf_ref, gf_ref, cw_ref,
                           n > 0, n < nb - 1, tb, bd_mask),
            _dn_block_prep(True, xb_ref, xbp_ref, xbn_ref, bb_ref, gb_ref, cw_ref,
                           n < nb - 1, n > 0, tb, bd_mask))
    units = [(d, t if d == 0 else nch - 1 - t) for t in range(nch) for d in range(2)]

    st = []
    for d, ci in units:
        p = prep[d]
        rows = slice(ci * c, (ci + 1) * c)
        qc, kc, vc, bc, gc = p["q"][rows], p["k"][rows], p["v"][rows], p["beta"][rows], p["gc"][rows]
        gc_row = jnp.sum(jnp.where(p["row_tri"], p["g"][rows], 0.0), axis=0, keepdims=True)
        decay = jnp.where(p["tri"], jnp.exp(jnp.where(p["tri"], gc - gc_row, 0.0)), 0.0)
        g_end = gc[0:1] if d == 1 else gc[c - 1:c]
        eg = jnp.exp(gc)
        kb = kc.astype(BF16)
        gram = lax.dot_general(jnp.concatenate([qc.astype(BF16), kb], axis=0),
                               _block_diag(kb, bd_mask),
                               (((1,), (1,)), ((), ())), preferred_element_type=F32)
        lmat = jnp.where(p["strict"], gram[c:] * bc * decay, 0.0)
        st.append(dict(a_intra=(gram[:c] * decay).astype(BF16), lmat=lmat,
                       tinv=p["eye"] - jnp.where(p["lvl_masks"][0], lmat, 0.0),
                       rhs_u=(vc * bc).astype(BF16), rhs_w=(kc * bc * eg).astype(BF16),
                       q_dec=qc * eg, k_dec=(kc * jnp.exp(g_end - gc)).astype(BF16),
                       gamma=jnp.exp(g_end)))

    for lvl in range(1, int(math.log2(c))):
        m1 = []
        for (d, ci), u in zip(units, st):
            loff = jnp.where(prep[d]["lvl_masks"][lvl], u["lmat"], 0.0).astype(BF16)
            m1.append(jnp.dot(loff, _block_diag(u["tinv"].astype(BF16), bd_mask),
                              preferred_element_type=F32))
        for u, m in zip(st, m1):
            u["tinv"] = u["tinv"] - jnp.dot(u["tinv"].astype(BF16),
                                            _block_diag(m.astype(BF16), bd_mask),
                                            preferred_element_type=F32)

    for u in st:
        uw = jnp.dot(u["tinv"].astype(BF16), bd2(u["rhs_u"], u["rhs_w"]),
                     preferred_element_type=F32)
        u["uw"] = uw.astype(BF16)
    for u in st:
        uwb = u["uw"]
        kuw = lax.dot_general(u["k_dec"], uwb, (((0,), (0,)), ((), ())),
                              preferred_element_type=F32)
        kuw = jnp.where(bd2_mask, kuw, 0.0)
        auw = jnp.dot(u["a_intra"], bd2(uwb[:, :w], uwb[:, w:]),
                      preferred_element_type=F32)
        u["b_bd"] = kuw[:, :w]
        u["lhs"] = jnp.concatenate([kuw[:, w:].astype(BF16),
                                    (u["q_dec"] - auw[:, w:]).astype(BF16)], axis=0)
        u["e"] = auw[:, :w]

    s_refs = (sf_ref, sb_ref)
    o_refs = (of_ref, ob_ref)
    state = [sf_ref[...], sb_ref[...]]
    for (d, ci), u in zip(units, st):
        r = jnp.dot(u["lhs"], state[d].astype(BF16), preferred_element_type=F32)
        o_refs[d][0, ci * c:(ci + 1) * c, :] = r[w:] + u["e"]
        state[d] = state[d] * u["gamma"] - r[:w] + u["b_bd"]
    for d in range(2):
        s_refs[d][...] = state[d]


def _deltanet(qkv, beta, g, cw, tb):
    b, s, _ = qkv.shape
    nb = s // tb
    hpb = tb // HALO
    nh = s // HALO
    w3 = 3 * DN_WIDTH
    fwd = lambda bi, n: (bi, n, 0)
    bwd = lambda bi, n: (bi, nb - 1 - n, 0)
    bwd1 = lambda bi, n: (bi, nb - 1 - n, 1)
    fwd_prev = lambda bi, n: (bi, jnp.maximum(n * hpb - 1, 0), 0)
    fwd_next = lambda bi, n: (bi, jnp.minimum((n + 1) * hpb, nh - 1), 0)
    bwd_prev = lambda bi, n: (bi, jnp.maximum((nb - 1 - n) * hpb - 1, 0), 0)
    bwd_next = lambda bi, n: (bi, jnp.minimum((nb - n) * hpb, nh - 1), 0)
    gate = lambda im: pl.BlockSpec((1, tb, DN_WIDTH), im)
    return pl.pallas_call(
        functools.partial(_deltanet_kernel, tb=tb),
        grid=(b, nb),
        in_specs=[
            pl.BlockSpec((1, tb, w3), fwd), pl.BlockSpec((1, HALO, w3), fwd_prev),
            pl.BlockSpec((1, HALO, w3), fwd_next),
            pl.BlockSpec((1, tb, w3), bwd), pl.BlockSpec((1, HALO, w3), bwd_prev),
            pl.BlockSpec((1, HALO, w3), bwd_next),
            gate(fwd), gate(fwd), gate(bwd1), gate(bwd1),
            pl.BlockSpec((8, w3), lambda bi, n: (0, 0)),
        ],
        out_specs=[pl.BlockSpec((1, tb, DN_WIDTH), fwd), pl.BlockSpec((1, tb, DN_WIDTH), bwd)],
        out_shape=[jax.ShapeDtypeStruct((b, s, DN_WIDTH), F32)] * 2,
        scratch_shapes=[pltpu.VMEM((DN_WIDTH, DN_WIDTH), F32)] * 2,
        compiler_params=pltpu.CompilerParams(
            dimension_semantics=("parallel", "arbitrary"), vmem_limit_bytes=VMEM_LIMIT),
        name="deltanet",
    )(qkv, qkv, qkv, qkv, qkv, qkv, beta, g, beta, g, cw)


def _diffattn_kernel(q_ref, k_ref, vt_ref, lam_ref, nw_ref, o_ref, *, lam_init, tks):
    d = DIFF_HEAD_DIM
    dv = 2 * d
    q = q_ref[0]
    tq = q.shape[0]
    lane = lax.broadcasted_iota(jnp.int32, q.shape, 1)
    zero = jnp.zeros((), q.dtype)
    q_half = (jnp.where(lane < d, q, zero), jnp.where(lane >= d, q, zero))

    tk = k_ref.shape[1]
    ones = jnp.ones((ONES_ROWS, tks), BF16)

    def scores(sub, j):
        kb = k_ref[0, sub * tks:(sub + 1) * tks, :]
        return lax.dot_general(kb, q_half[j], (((1,), (1,)), ((), ())),
                               preferred_element_type=F32)

    chains = [(sub, j) for sub in range(tk // tks) for j in range(2)]
    m = [jnp.full((1, tq), -jnp.inf, F32)] * 2
    acc = [jnp.zeros((dv + ONES_ROWS, tq), F32)] * 2
    s_next = scores(*chains[0])
    for idx, (sub, j) in enumerate(chains):
        s = s_next
        if idx + 1 < len(chains):
            s_next = scores(*chains[idx + 1])
        vt = jnp.concatenate([vt_ref[0, 0, :, sub * tks:(sub + 1) * tks], ones], axis=0)
        m_new = jnp.maximum(m[j], jnp.max(s, axis=0, keepdims=True))
        alpha = jnp.exp2(m[j] - m_new)
        p = jnp.exp2(s - m_new).astype(BF16)
        acc[j] = alpha * acc[j] + jnp.dot(vt, p, preferred_element_type=F32)
        m[j] = m_new

    lv = lam_ref[...]
    lam = (jnp.exp(jnp.sum(lv[0:1] * lv[1:2])) - jnp.exp(jnp.sum(lv[2:3] * lv[3:4]))
           + lam_init)
    a0, a1 = acc
    o = a0[:dv] / a0[dv:dv + 1] - lam * (a1[:dv] / a1[dv:dv + 1])
    ms = jnp.mean(o * o, axis=0, keepdims=True)
    o = o * lax.rsqrt(ms + EPS) * nw_ref[...] * (1.0 - lam_init)
    o_ref[0] = o.T.astype(o_ref.dtype)


def _diffattn(dq, dk, dvt, lam_p, nw_col, lam_init, tq, tks):
    b, s, _ = dq.shape
    hw = 2 * DIFF_HEAD_DIM
    return pl.pallas_call(
        functools.partial(_diffattn_kernel, lam_init=lam_init, tks=tks),
        grid=(b, DIFF_HEADS, s // tq),
        in_specs=[
            pl.BlockSpec((1, tq, hw), lambda bi, h, qi: (bi, qi, h)),
            pl.BlockSpec((1, s, hw), lambda bi, h, qi: (bi, 0, h)),
            pl.BlockSpec((1, 1, hw, s), lambda bi, h, qi: (bi, h, 0, 0)),
            pl.BlockSpec((4, DIFF_HEAD_DIM), lambda bi, h, qi: (0, 0)),
            pl.BlockSpec((hw, 1), lambda bi, h, qi: (0, 0)),
        ],
        out_specs=pl.BlockSpec((1, tq, hw), lambda bi, h, qi: (bi, qi, h)),
        out_shape=jax.ShapeDtypeStruct((b, s, DIFF_WIDTH), BF16),
        compiler_params=pltpu.CompilerParams(
            dimension_semantics=("parallel", "parallel", "parallel"),
            vmem_limit_bytes=VMEM_LIMIT),
        name="diffattn",
    )(dq, dk, dvt, lam_p, nw_col)


def _swa_kernel(q_ref, kp_ref, km_ref, kn_ref, vp_ref, vm_ref, vn_ref, sink_ref, o_ref, *, tq):
    qi = pl.program_id(1)
    g = pl.program_id(2)
    nq = pl.num_programs(1)
    d = SWA_HEAD_DIM
    grp = SWA_HEADS // SWA_KV_HEADS
    q = q_ref[0]
    k = jnp.concatenate([kp_ref[0], km_ref[0], kn_ref[0]], axis=0)
    v = jnp.concatenate([vp_ref[0], vm_ref[0], vn_ref[0]], axis=0)
    nk = tq + 2 * WINDOW
    qpos = lax.broadcasted_iota(jnp.int32, (tq, nk), 0)
    koff = lax.broadcasted_iota(jnp.int32, (tq, nk), 1) - WINDOW
    valid = jnp.abs(koff - qpos) <= WINDOW
    valid &= (koff >= 0) | (qi > 0)
    valid &= (koff < tq) | (qi < nq - 1)
    lane = lax.broadcasted_iota(jnp.int32, q.shape, 1)
    zero = jnp.zeros((), q.dtype)
    out = jnp.zeros((tq, grp * d), F32)
    for j in range(grp):
        in_head = (lane >= j * d) & (lane < (j + 1) * d)
        s = lax.dot_general(jnp.where(in_head, q, zero), k, (((1,), (1,)), ((), ())),
                            preferred_element_type=F32)
        s = jnp.where(valid, s, -jnp.inf)
        sink = sink_ref[g * grp + j]
        m = jnp.maximum(jnp.max(s, axis=-1, keepdims=True), sink)
        p = jnp.exp(s - m)
        denom = jnp.sum(p, axis=-1, keepdims=True) + jnp.exp(sink - m)
        pv = jnp.dot(p.astype(BF16), v, preferred_element_type=F32)
        out = jnp.where(in_head, pv / denom, out)
    o_ref[0] = out.astype(o_ref.dtype)


def _swa(sq, sk, sv, sink, tq):
    b, s, _ = sq.shape
    nq = s // tq
    wpb = tq // WINDOW
    nw = s // WINDOW
    gw = (SWA_HEADS // SWA_KV_HEADS) * SWA_HEAD_DIM
    main = lambda bi, qi, g: (bi, qi, g)
    prev = lambda bi, qi, g: (bi, jnp.maximum(qi * wpb - 1, 0), g)
    nxt = lambda bi, qi, g: (bi, jnp.minimum((qi + 1) * wpb, nw - 1), g)
    return pl.pallas_call(
        functools.partial(_swa_kernel, tq=tq),
        grid=(b, nq, SWA_KV_HEADS),
        in_specs=[
            pl.BlockSpec((1, tq, gw), main),
            pl.BlockSpec((1, WINDOW, gw), prev), pl.BlockSpec((1, tq, gw), main),
            pl.BlockSpec((1, WINDOW, gw), nxt),
            pl.BlockSpec((1, WINDOW, gw), prev), pl.BlockSpec((1, tq, gw), main),
            pl.BlockSpec((1, WINDOW, gw), nxt),
            pl.BlockSpec(memory_space=pltpu.SMEM),
        ],
        out_specs=pl.BlockSpec((1, tq, gw), main),
        out_shape=jax.ShapeDtypeStruct((b, s, SWA_WIDTH), BF16),
        compiler_params=pltpu.CompilerParams(
            dimension_semantics=("parallel", "parallel", "parallel"),
            vmem_limit_bytes=VMEM_LIMIT),
        name="swa",
    )(sq, sk, sk, sk, sv, sv, sv, sink)


def _outmlp_kernel(x_ref, of_ref, ob_ref, z_ref, df_ref, sw_ref, dnw_ref, wo_ref,
                   n2_ref, wu_ref, wd_ref, fn_ref, y_ref, *, final_norm, ff_chunk):
    w = DN_WIDTH
    o = of_ref[...] + ob_ref[...]
    r_i = lax.broadcasted_iota(jnp.int32, (w, w), 0)
    l_i = lax.broadcasted_iota(jnp.int32, (w, w), 1)
    ones_bd = ((r_i // DN_HEAD_DIM) == (l_i // DN_HEAD_DIM)).astype(BF16)
    sq = o * o
    hi = sq.astype(BF16)
    lo = (sq - hi.astype(F32)).astype(BF16)
    ss = (jnp.dot(hi, ones_bd, preferred_element_type=F32)
          + jnp.dot(lo, ones_bd, preferred_element_type=F32))
    dn = o * lax.rsqrt(ss * (1.0 / DN_HEAD_DIM) + EPS) * dnw_ref[...]
    dn = dn * _silu(z_ref[...].astype(F32))
    x = x_ref[...]
    x = x + jnp.dot(dn.astype(BF16), wo_ref[0:w, :], preferred_element_type=F32)
    x = x + jnp.dot(df_ref[...], wo_ref[w:w + DIFF_WIDTH, :], preferred_element_type=F32)
    x = x + jnp.dot(sw_ref[...], wo_ref[w + DIFF_WIDTH:, :], preferred_element_type=F32)
    ms = jnp.mean(x * x, axis=-1, keepdims=True)
    hn = (x * lax.rsqrt(ms + EPS) * n2_ref[...]).astype(BF16)
    mlp = jnp.zeros_like(x)
    for c0 in range(0, D_FF, ff_chunk):
        h = jnp.dot(hn, wu_ref[:, c0:c0 + ff_chunk], preferred_element_type=F32)
        h = jnp.square(jnp.maximum(h, 0.0)).astype(BF16)
        mlp = mlp + jnp.dot(h, wd_ref[c0:c0 + ff_chunk, :], preferred_element_type=F32)
    x = x + mlp
    if final_norm:
        ms = jnp.mean(x * x, axis=-1, keepdims=True)
        x = x * lax.rsqrt(ms + EPS) * fn_ref[...]
    y_ref[...] = x


def _outmlp(x2d, o_f, o_b, z, df, sw, dnw_e, wo, n2, wu, wd, fn, final_norm, tm):
    t, d = x2d.shape
    row = lambda i: (i, 0)
    const = lambda i: (0, 0)
    resident = lambda shape: pl.BlockSpec(shape, const)
    return pl.pallas_call(
        functools.partial(_outmlp_kernel, final_norm=final_norm, ff_chunk=1024),
        grid=(t // tm,),
        in_specs=[
            pl.BlockSpec((tm, d), row),
            pl.BlockSpec((tm, DN_WIDTH), row), pl.BlockSpec((tm, DN_WIDTH), row),
            pl.BlockSpec((tm, DN_WIDTH), row), pl.BlockSpec((tm, DIFF_WIDTH), row),
            pl.BlockSpec((tm, SWA_WIDTH), row),
            pl.BlockSpec((1, DN_WIDTH), const),
            resident((d, d)), pl.BlockSpec((1, d), const),
            resident((d, D_FF)), resident((D_FF, d)),
            pl.BlockSpec((1, d), const),
        ],
        out_specs=pl.BlockSpec((tm, d), row),
        out_shape=jax.ShapeDtypeStruct((t, d), F32),
        compiler_params=pltpu.CompilerParams(
            dimension_semantics=("parallel",), vmem_limit_bytes=VMEM_LIMIT),
        name="outmlp",
    )(x2d, o_f, o_b, z, df, sw, dnw_e, wo, n2, wu, wd, fn)


def _pick(n, pref):
    t = min(n, pref)
    assert n % t == 0, (n, pref)
    return t


def _trunk(x, params):
    b, s, d = x.shape
    assert d == D_MODEL and s % DN_CHUNK == 0 and s % WINDOW == 0
    depth = params["w_in_r"].shape[0]
    tm = _pick(s, 512)
    cos_t, sin_t = _rope_tables(s)
    x2d = x.reshape(b * s, d)
    for li in range(depth):
        p = {k: v[li] for k, v in params.items() if k != "final_norm_w"}
        lam_init = 0.8 - 0.6 * math.exp(-0.3 * li)
        qkv, z, beta, g, dq, dk, dv, sq, sk, sv = _inproj(
            x2d, s, p["norm1_w"], p["w_in_r"], cos_t, sin_t, p["alog_e"], p["dtb_e"], tm)
        r3 = lambda a: a.reshape(b, s, a.shape[-1])
        o_f, o_b = _deltanet(r3(qkv), r3(beta), r3(g), p["conv_w"], _pick(s, 256))
        df = _diffattn(r3(dq), r3(dk), dv, p["diff_lambda"], p["diff_norm_w"], lam_init,
                       _pick(s, 512), _pick(s, 1024))
        sw = _swa(r3(sq), r3(sk), r3(sv), p["swa_sink"], _pick(s, 512))
        r2 = lambda a: a.reshape(b * s, a.shape[-1])
        x2d = _outmlp(x2d, r2(o_f), r2(o_b), z, r2(df), r2(sw), p["dn_norm_e"], p["w_out"],
                      p["norm2_w"], p["w_up"], p["w_down"], params["final_norm_w"],
                      li == depth - 1, tm)
    return x2d.reshape(b, s, d)


def kernel(x_prompt, x_sample, norm1_w, w_in, dn_conv_w, dn_a_log, dn_dt_bias, dn_norm_w,
           diff_lambda, diff_norm_w, swa_sink, w_out, norm2_w, w_up, w_down, final_norm_w):
    depth = w_in.shape[0]
    rep = lambda a: jnp.repeat(a.astype(F32).reshape(depth, 1, 2 * DN_HEADS), DN_HEAD_DIM, axis=2)
    params = {
        "norm1_w": norm1_w.astype(F32)[:, None, :],
        "w_in_r": _reorder_w_in(w_in),
        "conv_w": jnp.pad(dn_conv_w.astype(F32), ((0, 0), (0, 8 - DN_CONV), (0, 0))),
        "alog_e": rep(dn_a_log),
        "dtb_e": rep(dn_dt_bias),
        "dn_norm_e": jnp.tile(dn_norm_w.astype(F32), (1, DN_HEADS))[:, None, :],
        "diff_lambda": diff_lambda.astype(F32),
        "diff_norm_w": diff_norm_w.astype(F32)[:, :, None],
        "swa_sink": swa_sink.astype(F32),
        "w_out": w_out.astype(BF16),
        "norm2_w": norm2_w.astype(F32)[:, None, :],
        "w_up": w_up.astype(BF16),
        "w_down": w_down.astype(BF16),
        "final_norm_w": final_norm_w.astype(F32)[None, :],
    }
    return _trunk(x_prompt, params), _trunk(x_sample, params)
```

```python
import functools
import math

import numpy as np
import jax
import jax.numpy as jnp
from jax import lax
from jax.experimental import pallas as pl
from jax.experimental.pallas import tpu as pltpu

F32 = jnp.float32
BF16 = jnp.bfloat16

D_MODEL = 1024
DN_HEADS = 4
DN_HEAD_DIM = 64
DN_WIDTH = DN_HEADS * DN_HEAD_DIM
DN_CONV = 5
DN_CHUNK = 64
DIFF_HEADS = 4
DIFF_HEAD_DIM = 64
DIFF_WIDTH = DIFF_HEADS * 2 * DIFF_HEAD_DIM
SWA_HEADS = 4
SWA_KV_HEADS = 2
SWA_HEAD_DIM = 64
SWA_WIDTH = SWA_HEADS * SWA_HEAD_DIM
WINDOW = 128
ROPE_THETA = 10000.0
D_FF = 4 * D_MODEL
EPS = 1e-6

LANES = 128
HALO = 16
ONES_ROWS = 16
LOG2E = math.log2(math.e)
VMEM_LIMIT = 56 * 1024 * 1024

_SEC_SIZES = (3 * DN_WIDTH, DN_WIDTH, 2 * DN_WIDTH, 2 * DN_WIDTH,
              DIFF_WIDTH, DIFF_WIDTH, DIFF_WIDTH, SWA_WIDTH, SWA_WIDTH, SWA_WIDTH)
_SEC_OFFS = tuple(int(v) for v in np.cumsum((0,) + _SEC_SIZES))
IN_COLS_R = _SEC_OFFS[-1]


def _reorder_w_in(w_in):
    depth, d, _ = w_in.shape
    w = w_in.astype(BF16)
    o_b = 4 * DN_WIDTH
    o_dq = o_b + 4 * DN_HEADS
    o_sk = o_dq + 3 * DIFF_WIDTH + SWA_WIDTH
    kv_w = SWA_KV_HEADS * SWA_HEAD_DIM
    grp = SWA_HEADS // SWA_KV_HEADS

    def per_group(cols):
        c = cols.reshape(depth, d, SWA_KV_HEADS, 1, SWA_HEAD_DIM)
        return jnp.broadcast_to(c, (depth, d, SWA_KV_HEADS, grp, SWA_HEAD_DIM)).reshape(depth, d, -1)

    out = jnp.concatenate([
        w[:, :, :o_b],
        jnp.repeat(w[:, :, o_b:o_dq], DN_HEAD_DIM, axis=2),
        w[:, :, o_dq:o_sk],
        per_group(w[:, :, o_sk:o_sk + kv_w]),
        per_group(w[:, :, o_sk + kv_w:o_sk + 2 * kv_w])], axis=2)
    assert out.shape[2] == IN_COLS_R
    return out


def _rope_tables(seq):
    half = DIFF_HEAD_DIM // 2
    inv = ROPE_THETA ** (-jnp.arange(half, dtype=F32) / half)
    ang = jnp.arange(seq, dtype=F32)[:, None] * inv[None, :]
    cos, sin = jnp.cos(ang), jnp.sin(ang)
    reps = LANES // (2 * half)
    cos_t = jnp.tile(jnp.concatenate([cos, cos], axis=1), (1, reps))
    sin_t = jnp.tile(jnp.concatenate([-sin, sin], axis=1), (1, reps))
    return cos_t, sin_t


def _sigmoid(x):
    return 1.0 / (1.0 + jnp.exp(-x))


def _silu(x):
    return x * _sigmoid(x)


def _inproj_kernel(x_ref, nw_ref, w_ref, cos_ref, sin_ref, alog_ref, dtb_ref,
                   qkv_ref, z_ref, beta_ref, g_ref, dq_ref, dk_ref, dv_ref,
                   sq_ref, sk_ref, sv_ref):
    x = x_ref[...]
    ms = jnp.mean(x * x, axis=-1, keepdims=True)
    hn = (x * lax.rsqrt(ms + EPS) * nw_ref[...]).astype(BF16)
    cos = cos_ref[...]
    sin = sin_ref[...]
    lane = lax.broadcasted_iota(jnp.int32, cos.shape, 1)
    low = (lane % DIFF_HEAD_DIM) < (DIFF_HEAD_DIM // 2)

    def proj(sec):
        return jnp.dot(hn, w_ref[:, _SEC_OFFS[sec]:_SEC_OFFS[sec + 1]],
                       preferred_element_type=F32)

    def rope_store(sec, out_ref, scale):
        y_all = proj(sec)
        for c0 in range(0, _SEC_SIZES[sec], LANES):
            y = y_all[:, c0:c0 + LANES]
            up = pltpu.roll(y, DIFF_HEAD_DIM // 2, axis=1)
            down = pltpu.roll(y, LANES - DIFF_HEAD_DIM // 2, axis=1)
            r = y * cos + jnp.where(low, down, up) * sin
            if scale != 1.0:
                r = r * scale
            out_ref[:, c0:c0 + LANES] = r.astype(out_ref.dtype)

    qkv_ref[...] = proj(0).astype(qkv_ref.dtype)
    z_ref[...] = proj(1).astype(z_ref.dtype)
    beta_ref[...] = _sigmoid(proj(2))
    a = proj(3) + dtb_ref[...]
    softplus = jnp.maximum(a, 0.0) + jnp.log(1.0 + jnp.exp(-jnp.abs(a)))
    g_ref[...] = -jnp.exp(alog_ref[...]) * softplus
    rope_store(4, dq_ref, DIFF_HEAD_DIM ** -0.5 * LOG2E)
    rope_store(5, dk_ref, 1.0)
    dv = proj(6)
    hw = 2 * DIFF_HEAD_DIM
    for h in range(DIFF_HEADS):
        dv_ref[0, h] = dv[:, h * hw:(h + 1) * hw].T.astype(dv_ref.dtype)
    rope_store(7, sq_ref, SWA_HEAD_DIM ** -0.5)
    rope_store(8, sk_ref, 1.0)
    sv_ref[...] = proj(9).astype(sv_ref.dtype)


def _inproj(x2d, seq, nw, w_r, cos_t, sin_t, alog_e, dtb_e, tm):
    t, d = x2d.shape
    nseq = seq // tm
    row = lambda i: (i, 0)
    const = lambda i: (0, 0)
    widths = (3 * DN_WIDTH, DN_WIDTH, 2 * DN_WIDTH, 2 * DN_WIDTH, DIFF_WIDTH, DIFF_WIDTH,
              DIFF_WIDTH, SWA_WIDTH, SWA_WIDTH, SWA_WIDTH)
    dtypes = (BF16, BF16, F32, F32, BF16, BF16, BF16, BF16, BF16, BF16)
    dvt_pos = 6
    hw = 2 * DIFF_HEAD_DIM
    dvt_spec = pl.BlockSpec((1, DIFF_HEADS, hw, tm), lambda i: (i // nseq, 0, 0, i % nseq))
    dvt_shape = jax.ShapeDtypeStruct((t // seq, DIFF_HEADS, hw, seq), BF16)
    return pl.pallas_call(
        _inproj_kernel,
        grid=(t // tm,),
        in_specs=[
            pl.BlockSpec((tm, d), row),
            pl.BlockSpec((1, d), const),
            pl.BlockSpec((d, IN_COLS_R), const, pipeline_mode=pl.Buffered(1)),
            pl.BlockSpec((tm, LANES), lambda i: (i % nseq, 0)),
            pl.BlockSpec((tm, LANES), lambda i: (i % nseq, 0)),
            pl.BlockSpec((1, 2 * DN_WIDTH), const),
            pl.BlockSpec((1, 2 * DN_WIDTH), const),
        ],
        out_specs=[dvt_spec if k == dvt_pos else pl.BlockSpec((tm, w), row)
                   for k, w in enumerate(widths)],
        out_shape=[dvt_shape if k == dvt_pos else jax.ShapeDtypeStruct((t, w), dt)
                   for k, (w, dt) in enumerate(zip(widths, dtypes))],
        compiler_params=pltpu.CompilerParams(
            dimension_semantics=("parallel",), vmem_limit_bytes=VMEM_LIMIT),
        name="inproj",
    )(x2d, nw, w_r, cos_t, sin_t, alog_e, dtb_e)


def _block_diag(m, bd_mask):
    return jnp.where(bd_mask, jnp.concatenate([m] * DN_HEADS, axis=0), jnp.zeros((), m.dtype))


def _dn_block_prep(reverse, x_ref, xp_ref, xn_ref, b_ref, g_ref, cw_ref, has_prev, has_next,
                   tb, bd_mask):
    c = DN_CHUNK
    w = DN_WIDTH
    xm = x_ref[0].astype(F32)
    xp = jnp.where(has_prev, xp_ref[0].astype(F32), 0.0)
    xn = jnp.where(has_next, xn_ref[0].astype(F32), 0.0)
    ext = jnp.concatenate([xp, xm, xn], axis=0)
    n_ext = tb + 2 * HALO
    cw = cw_ref[...]
    y = jnp.zeros_like(xm)
    for tap in range(DN_CONV):
        off = tap - DN_CONV // 2
        if off == 0:
            sh = xm
        else:
            sh = pltpu.roll(ext, (-off) % n_ext, axis=0)[HALO:HALO + tb]
        y = y + sh * cw[tap:tap + 1, :]
    y = _silu(y)
    q, k, v = y[:, :w], y[:, w:2 * w], y[:, 2 * w:]

    ones_bd = bd_mask.astype(BF16)

    def seg_sum(t):
        hi = t.astype(BF16)
        lo = (t - hi.astype(F32)).astype(BF16)
        return (jnp.dot(hi, ones_bd, preferred_element_type=F32)
                + jnp.dot(lo, ones_bd, preferred_element_type=F32))

    q = q * lax.rsqrt(seg_sum(q * q) + EPS) * (DN_HEAD_DIM ** -0.5)
    k = k * lax.rsqrt(seg_sum(k * k) + EPS)

    g = g_ref[0]

    i_c = lax.broadcasted_iota(jnp.int32, (c, w), 0)
    j_c = lax.broadcasted_iota(jnp.int32, (c, w), 1) % DN_HEAD_DIM
    if reverse:
        tri, strict, row_tri = i_c <= j_c, i_c < j_c, i_c >= j_c
    else:
        tri, strict, row_tri = i_c >= j_c, i_c > j_c, i_c <= j_c
    lvl_masks = []
    for lvl in range(int(math.log2(c))):
        siblings = (i_c >> (lvl + 1)) == (j_c >> (lvl + 1))
        odd_i = ((i_c >> lvl) & 1) == 1
        odd_j = ((j_c >> lvl) & 1) == 1
        lvl_masks.append(siblings & ((~odd_i & odd_j) if reverse else (odd_i & ~odd_j)))

    r_b = lax.broadcasted_iota(jnp.int32, (tb, tb), 0)
    c_b = lax.broadcasted_iota(jnp.int32, (tb, tb), 1)
    same = (r_b // c) == (c_b // c)
    cum = (same & ((r_b <= c_b) if reverse else (r_b >= c_b))).astype(F32)
    gc_all = jnp.dot(cum, g, preferred_element_type=F32, precision=lax.Precision.HIGHEST)
    return dict(q=q, k=k, v=v, beta=b_ref[0], g=g, gc=gc_all, tri=tri, strict=strict,
                row_tri=row_tri, eye=(i_c == j_c).astype(F32), lvl_masks=lvl_masks)


def _deltanet_kernel(xf_ref, xfp_ref, xfn_ref, xb_ref, xbp_ref, xbn_ref,
                     bf_ref, gf_ref, bb_ref, gb_ref, cw_ref,
                     of_ref, ob_ref, sf_ref, sb_ref, *, tb):
    n = pl.program_id(1)
    nb = pl.num_programs(1)
    c = DN_CHUNK
    w = DN_WIDTH
    nch = tb // c

    @pl.when(n == 0)
    def _():
        sf_ref[...] = jnp.zeros_like(sf_ref)
        sb_ref[...] = jnp.zeros_like(sb_ref)

    r_i = lax.broadcasted_iota(jnp.int32, (w, w), 0)
    l_i = lax.broadcasted_iota(jnp.int32, (w, w), 1)
    bd_mask = (r_i // DN_HEAD_DIM) == (l_i // DN_HEAD_DIM)
    bd2_mask = jnp.concatenate([bd_mask, bd_mask], axis=1)

    def bd2(a, b):
        return jnp.concatenate([_block_diag(a, bd_mask), _block_diag(b, bd_mask)], axis=1)

    prep = (_dn_block_prep(False, xf_ref, xfp_ref, xfn_ref, bf_ref, gf_ref, cw_ref,
                           n > 0, n < nb - 1, tb, bd_mask),
            _dn_block_prep(True, xb_ref, xbp_ref, xbn_ref, bb_ref, gb_ref, cw_ref,
                           n < nb - 1, n > 0, tb, bd_mask))
    units = [(d, t if d == 0 else nch - 1 - t) for t in range(nch) for d in range(2)]

    st = []
    for d, ci in units:
        p = prep[d]
        rows = slice(ci * c, (ci + 1) * c)
        qc, kc, vc, bc, gc = p["q"][rows], p["k"][rows], p["v"][rows], p["beta"][rows], p["gc"][rows]
        gc_row = jnp.sum(jnp.where(p["row_tri"], p["g"][rows], 0.0), axis=0, keepdims=True)
        decay = jnp.where(p["tri"], jnp.exp(jnp.where(p["tri"], gc - gc_row, 0.0)), 0.0)
        g_end = gc[0:1] if d == 1 else gc[c - 1:c]
        eg = jnp.exp(gc)
        kb = kc.astype(BF16)
        gram = lax.dot_general(jnp.concatenate([qc.astype(BF16), kb], axis=0),
                               _block_diag(kb, bd_mask),
                               (((1,), (1,)), ((), ())), preferred_element_type=F32)
        lmat = jnp.where(p["strict"], gram[c:] * bc * decay, 0.0)
        st.append(dict(a_intra=(gram[:c] * decay).astype(BF16), lmat=lmat,
                       tinv=p["eye"] - jnp.where(p["lvl_masks"][0], lmat, 0.0),
                       rhs_u=(vc * bc).astype(BF16), rhs_w=(kc * bc * eg).astype(BF16),
                       q_dec=qc * eg, k_dec=(kc * jnp.exp(g_end - gc)).astype(BF16),
                       gamma=jnp.exp(g_end)))

    for lvl in range(1, int(math.log2(c))):
        m1 = []
        for (d, ci), u in zip(units, st):
            loff = jnp.where(prep[d]["lvl_masks"][lvl], u["lmat"], 0.0).astype(BF16)
            m1.append(jnp.dot(loff, _block_diag(u["tinv"].astype(BF16), bd_mask),
                              preferred_element_type=F32))
        for u, m in zip(st, m1):
            u["tinv"] = u["tinv"] - jnp.dot(u["tinv"].astype(BF16),
                                            _block_diag(m.astype(BF16), bd_mask),
                                            preferred_element_type=F32)

    for u in st:
        uw = jnp.dot(u["tinv"].astype(BF16), bd2(u["rhs_u"], u["rhs_w"]),
                     preferred_element_type=F32)
        u["uw"] = uw.astype(BF16)
    for u in st:
        uwb = u["uw"]
        kuw = lax.dot_general(u["k_dec"], uwb, (((0,), (0,)), ((), ())),
                              preferred_element_type=F32)
        kuw = jnp.where(bd2_mask, kuw, 0.0)
        auw = jnp.dot(u["a_intra"], bd2(uwb[:, :w], uwb[:, w:]),
                      preferred_element_type=F32)
        u["b_bd"] = kuw[:, :w]
        u["lhs"] = jnp.concatenate([kuw[:, w:].astype(BF16),
                                    (u["q_dec"] - auw[:, w:]).astype(BF16)], axis=0)
        u["e"] = auw[:, :w]

    s_refs = (sf_ref, sb_ref)
    o_refs = (of_ref, ob_ref)
    state = [sf_ref[...], sb_ref[...]]
    for (d, ci), u in zip(units, st):
        r = jnp.dot(u["lhs"], state[d].astype(BF16), preferred_element_type=F32)
        o_refs[d][0, ci * c:(ci + 1) * c, :] = r[w:] + u["e"]
        state[d] = state[d] * u["gamma"] - r[:w] + u["b_bd"]
    for d in range(2):
        s_refs[d][...] = state[d]


def _deltanet(qkv, beta, g, cw, tb):
    b, s, _ = qkv.shape
    nb = s // tb
    hpb = tb // HALO
    nh = s // HALO
    w3 = 3 * DN_WIDTH
    fwd = lambda bi, n: (bi, n, 0)
    bwd = lambda bi, n: (bi, nb - 1 - n, 0)
    bwd1 = lambda bi, n: (bi, nb - 1 - n, 1)
    fwd_prev = lambda bi, n: (bi, jnp.maximum(n * hpb - 1, 0), 0)
    fwd_next = lambda bi, n: (bi, jnp.minimum((n + 1) * hpb, nh - 1), 0)
    bwd_prev = lambda bi, n: (bi, jnp.maximum((nb - 1 - n) * hpb - 1, 0), 0)
    bwd_next = lambda bi, n: (bi, jnp.minimum((nb - n) * hpb, nh - 1), 0)
    gate = lambda im: pl.BlockSpec((1, tb, DN_WIDTH), im)
    return pl.pallas_call(
        functools.partial(_deltanet_kernel, tb=tb),
        grid=(b, nb),
        in_specs=[
            pl.BlockSpec((1, tb, w3), fwd), pl.BlockSpec((1, HALO, w3), fwd_prev),
            pl.BlockSpec((1, HALO, w3), fwd_next),
            pl.BlockSpec((1, tb, w3), bwd), pl.BlockSpec((1, HALO, w3), bwd_prev),
            pl.BlockSpec((1, HALO, w3), bwd_next),
            gate(fwd), gate(fwd), gate(bwd1), gate(bwd1),
            pl.BlockSpec((8, w3), lambda bi, n: (0, 0)),
        ],
        out_specs=[pl.BlockSpec((1, tb, DN_WIDTH), fwd), pl.BlockSpec((1, tb, DN_WIDTH), bwd)],
        out_shape=[jax.ShapeDtypeStruct((b, s, DN_WIDTH), F32)] * 2,
        scratch_shapes=[pltpu.VMEM((DN_WIDTH, DN_WIDTH), F32)] * 2,
        compiler_params=pltpu.CompilerParams(
            dimension_semantics=("parallel", "arbitrary"), vmem_limit_bytes=VMEM_LIMIT),
        name="deltanet",
    )(qkv, qkv, qkv, qkv, qkv, qkv, beta, g, beta, g, cw)


def _diffattn_kernel(q_ref, k_ref, vt_ref, lam_ref, nw_ref, o_ref, *, lam_init, tks):
    d = DIFF_HEAD_DIM
    dv = 2 * d
    q = q_ref[0]
    tq = q.shape[0]
    lane = lax.broadcasted_iota(jnp.int32, q.shape, 1)
    zero = jnp.zeros((), q.dtype)
    q_half = (jnp.where(lane < d, q, zero), jnp.where(lane >= d, q, zero))

    tk = k_ref.shape[1]
    ones = jnp.ones((ONES_ROWS, tks), BF16)

    def scores(sub, j):
        kb = k_ref[0, sub * tks:(sub + 1) * tks, :]
        return lax.dot_general(kb, q_half[j], (((1,), (1,)), ((), ())),
                               preferred_element_type=F32)

    chains = [(sub, j) for sub in range(tk // tks) for j in range(2)]
    m = [jnp.full((1, tq), -jnp.inf, F32)] * 2
    acc = [jnp.zeros((dv + ONES_ROWS, tq), F32)] * 2
    s_next = scores(*chains[0])
    for idx, (sub, j) in enumerate(chains):
        s = s_next
        if idx + 1 < len(chains):
            s_next = scores(*chains[idx + 1])
        vt = jnp.concatenate([vt_ref[0, 0, :, sub * tks:(sub + 1) * tks], ones], axis=0)
        m_new = jnp.maximum(m[j], jnp.max(s, axis=0, keepdims=True))
        alpha = jnp.exp2(m[j] - m_new)
        p = jnp.exp2(s - m_new).astype(BF16)
        acc[j] = alpha * acc[j] + jnp.dot(vt, p, preferred_element_type=F32)
        m[j] = m_new

    lv = lam_ref[...]
    lam = (jnp.exp(jnp.sum(lv[0:1] * lv[1:2])) - jnp.exp(jnp.sum(lv[2:3] * lv[3:4]))
           + lam_init)
    a0, a1 = acc
    o = a0[:dv] / a0[dv:dv + 1] - lam * (a1[:dv] / a1[dv:dv + 1])
    ms = jnp.mean(o * o, axis=0, keepdims=True)
    o = o * lax.rsqrt(ms + EPS) * nw_ref[...] * (1.0 - lam_init)
    o_ref[0] = o.T.astype(o_ref.dtype)


def _diffattn(dq, dk, dvt, lam_p, nw_col, lam_init, tq, tks):
    b, s, _ = dq.shape
    hw = 2 * DIFF_HEAD_DIM
    return pl.pallas_call(
        functools.partial(_diffattn_kernel, lam_init=lam_init, tks=tks),
        grid=(b, DIFF_HEADS, s // tq),
        in_specs=[
            pl.BlockSpec((1, tq, hw), lambda bi, h, qi: (bi, qi, h)),
            pl.BlockSpec((1, s, hw), lambda bi, h, qi: (bi, 0, h)),
            pl.BlockSpec((1, 1, hw, s), lambda bi, h, qi: (bi, h, 0, 0)),
            pl.BlockSpec((4, DIFF_HEAD_DIM), lambda bi, h, qi: (0, 0)),
            pl.BlockSpec((hw, 1), lambda bi, h, qi: (0, 0)),
        ],
        out_specs=pl.BlockSpec((1, tq, hw), lambda bi, h, qi: (bi, qi, h)),
        out_shape=jax.ShapeDtypeStruct((b, s, DIFF_WIDTH), BF16),
        compiler_params=pltpu.CompilerParams(
            dimension_semantics=("parallel", "parallel", "parallel"),
            vmem_limit_bytes=VMEM_LIMIT),
        name="diffattn",
    )(dq, dk, dvt, lam_p, nw_col)


def _swa_kernel(q_ref, kp_ref, km_ref, kn_ref, vp_ref, vm_ref, vn_ref, sink_ref, o_ref, *, tq):
    qi = pl.program_id(1)
    g = pl.program_id(2)
    nq = pl.num_programs(1)
    d = SWA_HEAD_DIM
    grp = SWA_HEADS // SWA_KV_HEADS
    q = q_ref[0]
    k = jnp.concatenate([kp_ref[0], km_ref[0], kn_ref[0]], axis=0)
    v = jnp.concatenate([vp_ref[0], vm_ref[0], vn_ref[0]], axis=0)
    nk = tq + 2 * WINDOW
    qpos = lax.broadcasted_iota(jnp.int32, (tq, nk), 0)
    koff = lax.broadcasted_iota(jnp.int32, (tq, nk), 1) - WINDOW
    valid = jnp.abs(koff - qpos) <= WINDOW
    valid &= (koff >= 0) | (qi > 0)
    valid &= (koff < tq) | (qi < nq - 1)
    lane = lax.broadcasted_iota(jnp.int32, q.shape, 1)
    zero = jnp.zeros((), q.dtype)
    out = jnp.zeros((tq, grp * d), F32)
    for j in range(grp):
        in_head = (lane >= j * d) & (lane < (j + 1) * d)
        s = lax.dot_general(jnp.where(in_head, q, zero), k, (((1,), (1,)), ((), ())),
                            preferred_element_type=F32)
        s = jnp.where(valid, s, -jnp.inf)
        sink = sink_ref[g * grp + j]
        m = jnp.maximum(jnp.max(s, axis=-1, keepdims=True), sink)
        p = jnp.exp(s - m)
        denom = jnp.sum(p, axis=-1, keepdims=True) + jnp.exp(sink - m)
        pv = jnp.dot(p.astype(BF16), v, preferred_element_type=F32)
        out = jnp.where(in_head, pv / denom, out)
    o_ref[0] = out.astype(o_ref.dtype)


def _swa(sq, sk, sv, sink, tq):
    b, s, _ = sq.shape
    nq = s // tq
    wpb = tq // WINDOW
    nw = s // WINDOW
    gw = (SWA_HEADS // SWA_KV_HEADS) * SWA_HEAD_DIM
    main = lambda bi, qi, g: (bi, qi, g)
    prev = lambda bi, qi, g: (bi, jnp.maximum(qi * wpb - 1, 0), g)
    nxt = lambda bi, qi, g: (bi, jnp.minimum((qi + 1) * wpb, nw - 1), g)
    return pl.pallas_call(
        functools.partial(_swa_kernel, tq=tq),
        grid=(b, nq, SWA_KV_HEADS),
        in_specs=[
            pl.BlockSpec((1, tq, gw), main),
            pl.BlockSpec((1, WINDOW, gw), prev), pl.BlockSpec((1, tq, gw), main),
            pl.BlockSpec((1, WINDOW, gw), nxt),
            pl.BlockSpec((1, WINDOW, gw), prev), pl.BlockSpec((1, tq, gw), main),
            pl.BlockSpec((1, WINDOW, gw), nxt),
            pl.BlockSpec(memory_space=pltpu.SMEM),
        ],
        out_specs=pl.BlockSpec((1, tq, gw), main),
        out_shape=jax.ShapeDtypeStruct((b, s, SWA_WIDTH), BF16),
        compiler_params=pltpu.CompilerParams(
            dimension_semantics=("parallel", "parallel", "parallel"),
            vmem_limit_bytes=VMEM_LIMIT),
        name="swa",
    )(sq, sk, sk, sk, sv, sv, sv, sink)


def _outmlp_kernel(x_ref, of_ref, ob_ref, z_ref, df_ref, sw_ref, dnw_ref, wo_ref,
                   n2_ref, wu_ref, wd_ref, fn_ref, y_ref, *, final_norm, ff_chunk):
    w = DN_WIDTH
    o = of_ref[...] + ob_ref[...]
    r_i = lax.broadcasted_iota(jnp.int32, (w, w), 0)
    l_i = lax.broadcasted_iota(jnp.int32, (w, w), 1)
    ones_bd = ((r_i // DN_HEAD_DIM) == (l_i // DN_HEAD_DIM)).astype(BF16)
    sq = o * o
    hi = sq.astype(BF16)
    lo = (sq - hi.astype(F32)).astype(BF16)
    ss = (jnp.dot(hi, ones_bd, preferred_element_type=F32)
          + jnp.dot(lo, ones_bd, preferred_element_type=F32))
    dn = o * lax.rsqrt(ss * (1.0 / DN_HEAD_DIM) + EPS) * dnw_ref[...]
    dn = dn * _silu(z_ref[...].astype(F32))
    x = x_ref[...]
    x = x + jnp.dot(dn.astype(BF16), wo_ref[0:w, :], preferred_element_type=F32)
    x = x + jnp.dot(df_ref[...], wo_ref[w:w + DIFF_WIDTH, :], preferred_element_type=F32)
    x = x + jnp.dot(sw_ref[...], wo_ref[w + DIFF_WIDTH:, :], preferred_element_type=F32)
    ms = jnp.mean(x * x, axis=-1, keepdims=True)
    hn = (x * lax.rsqrt(ms + EPS) * n2_ref[...]).astype(BF16)
    mlp = jnp.zeros_like(x)
    for c0 in range(0, D_FF, ff_chunk):
        h = jnp.dot(hn, wu_ref[:, c0:c0 + ff_chunk], preferred_element_type=F32)
        h = jnp.square(jnp.maximum(h, 0.0)).astype(BF16)
        mlp = mlp + jnp.dot(h, wd_ref[c0:c0 + ff_chunk, :], preferred_element_type=F32)
    x = x + mlp
    if final_norm:
        ms = jnp.mean(x * x, axis=-1, keepdims=True)
        x = x * lax.rsqrt(ms + EPS) * fn_ref[...]
    y_ref[...] = x


def _outmlp(x2d, o_f, o_b, z, df, sw, dnw_e, wo, n2, wu, wd, fn, final_norm, tm):
    t, d = x2d.shape
    row = lambda i: (i, 0)
    const = lambda i: (0, 0)
    resident = lambda shape: pl.BlockSpec(shape, const)
    return pl.pallas_call(
        functools.partial(_outmlp_kernel, final_norm=final_norm, ff_chunk=1024),
        grid=(t // tm,),
        in_specs=[
            pl.BlockSpec((tm, d), row),
            pl.BlockSpec((tm, DN_WIDTH), row), pl.BlockSpec((tm, DN_WIDTH), row),
            pl.BlockSpec((tm, DN_WIDTH), row), pl.BlockSpec((tm, DIFF_WIDTH), row),
            pl.BlockSpec((tm, SWA_WIDTH), row),
            pl.BlockSpec((1, DN_WIDTH), const),
            resident((d, d)), pl.BlockSpec((1, d), const),
            resident((d, D_FF)), resident((D_FF, d)),
            pl.BlockSpec((1, d), const),
        ],
        out_specs=pl.BlockSpec((tm, d), row),
        out_shape=jax.ShapeDtypeStruct((t, d), F32),
        compiler_params=pltpu.CompilerParams(
            dimension_semantics=("parallel",), vmem_limit_bytes=VMEM_LIMIT),
        name="outmlp",
    )(x2d, o_f, o_b, z, df, sw, dnw_e, wo, n2, wu, wd, fn)


def _pick(n, pref):
    t = min(n, pref)
    assert n % t == 0, (n, pref)
    return t


def _trunk(x, params):
    b, s, d = x.shape
    assert d == D_MODEL and s % DN_CHUNK == 0 and s % WINDOW == 0
    depth = params["w_in_r"].shape[0]
    tm = _pick(s, 512)
    cos_t, sin_t = _rope_tables(s)
    x2d = x.reshape(b * s, d)
    for li in range(depth):
        p = {k: v[li] for k, v in params.items() if k != "final_norm_w"}
        lam_init = 0.8 - 0.6 * math.exp(-0.3 * li)
        qkv, z, beta, g, dq, dk, dv, sq, sk, sv = _inproj(
            x2d, s, p["norm1_w"], p["w_in_r"], cos_t, sin_t, p["alog_e"], p["dtb_e"], tm)
        r3 = lambda a: a.reshape(b, s, a.shape[-1])
        o_f, o_b = _deltanet(r3(qkv), r3(beta), r3(g), p["conv_w"], _pick(s, 256))
        df = _diffattn(r3(dq), r3(dk), dv, p["diff_lambda"], p["diff_norm_w"], lam_init,
                       _pick(s, 1024), _pick(s, 512))
        sw = _swa(r3(sq), r3(sk), r3(sv), p["swa_sink"], _pick(s, 512))
        r2 = lambda a: a.reshape(b * s, a.shape[-1])
        x2d = _outmlp(x2d, r2(o_f), r2(o_b), z, r2(df), r2(sw), p["dn_norm_e"], p["w_out"],
                      p["norm2_w"], p["w_up"], p["w_down"], params["final_norm_w"],
                      li == depth - 1, tm)
    return x2d.reshape(b, s, d)


def kernel(x_prompt, x_sample, norm1_w, w_in, dn_conv_w, dn_a_log, dn_dt_bias, dn_norm_w,
           diff_lambda, diff_norm_w, swa_sink, w_out, norm2_w, w_up, w_down, final_norm_w):
    depth = w_in.shape[0]
    rep = lambda a: jnp.repeat(a.astype(F32).reshape(depth, 1, 2 * DN_HEADS), DN_HEAD_DIM, axis=2)
    params = {
        "norm1_w": norm1_w.astype(F32)[:, None, :],
        "w_in_r": _reorder_w_in(w_in),
        "conv_w": jnp.pad(dn_conv_w.astype(F32), ((0, 0), (0, 8 - DN_CONV), (0, 0))),
        "alog_e": rep(dn_a_log),
        "dtb_e": rep(dn_dt_bias),
        "dn_norm_e": jnp.tile(dn_norm_w.astype(F32), (1, DN_HEADS))[:, None, :],
        "diff_lambda": diff_lambda.astype(F32),
        "diff_norm_w": diff_norm_w.astype(F32)[:, :, None],
        "swa_sink": swa_sink.astype(F32),
        "w_out": w_out.astype(BF16),
        "norm2_w": norm2_w.astype(F32)[:, None, :],
        "w_up": w_up.astype(BF16),
        "w_down": w_down.astype(BF16),
        "final_norm_w": final_norm_w.astype(F32)[None, :],
    }
    return _trunk(x_prompt, params), _trunk(x_sample, params)
```

```python
import functools
import math

import numpy as np
import jax
import jax.numpy as jnp
from jax import lax
from jax.experimental import pallas as pl
from jax.experimental.pallas import tpu as pltpu

F32 = jnp.float32
BF16 = jnp.bfloat16

D_MODEL = 1024
DN_HEADS = 4
DN_HEAD_DIM = 64
DN_WIDTH = DN_HEADS * DN_HEAD_DIM
DN_CONV = 5
DN_CHUNK = 64
DIFF_HEADS = 4
DIFF_HEAD_DIM = 64
DIFF_WIDTH = DIFF_HEADS * 2 * DIFF_HEAD_DIM
SWA_HEADS = 4
SWA_KV_HEADS = 2
SWA_HEAD_DIM = 64
SWA_WIDTH = SWA_HEADS * SWA_HEAD_DIM
WINDOW = 128
ROPE_THETA = 10000.0
D_FF = 4 * D_MODEL
EPS = 1e-6

LANES = 128
CONV_HALO = 8
ONES_ROWS = 16
LOG2E = math.log2(math.e)
VMEM_LIMIT = 56 * 1024 * 1024

_SEC_SIZES = (3 * DN_WIDTH, DN_WIDTH, 2 * DN_WIDTH, 2 * DN_WIDTH,
              DIFF_WIDTH, DIFF_WIDTH, DIFF_WIDTH, SWA_WIDTH, SWA_WIDTH, SWA_WIDTH)
_SEC_OFFS = tuple(int(v) for v in np.cumsum((0,) + _SEC_SIZES))
IN_COLS_R = _SEC_OFFS[-1]


def _reorder_w_in(w_in):
    depth, d, _ = w_in.shape
    w = w_in.astype(BF16)
    o_b = 4 * DN_WIDTH
    o_dq = o_b + 4 * DN_HEADS
    o_sk = o_dq + 3 * DIFF_WIDTH + SWA_WIDTH
    kv_w = SWA_KV_HEADS * SWA_HEAD_DIM
    grp = SWA_HEADS // SWA_KV_HEADS

    def per_group(cols):
        c = cols.reshape(depth, d, SWA_KV_HEADS, 1, SWA_HEAD_DIM)
        return jnp.broadcast_to(c, (depth, d, SWA_KV_HEADS, grp, SWA_HEAD_DIM)).reshape(depth, d, -1)

    out = jnp.concatenate([
        w[:, :, :o_b],
        jnp.repeat(w[:, :, o_b:o_dq], DN_HEAD_DIM, axis=2),
        w[:, :, o_dq:o_sk],
        per_group(w[:, :, o_sk:o_sk + kv_w]),
        per_group(w[:, :, o_sk + kv_w:o_sk + 2 * kv_w])], axis=2)
    assert out.shape[2] == IN_COLS_R
    return out


def _rope_tables(seq):
    half = DIFF_HEAD_DIM // 2
    inv = ROPE_THETA ** (-jnp.arange(half, dtype=F32) / half)
    ang = jnp.arange(seq, dtype=F32)[:, None] * inv[None, :]
    cos, sin = jnp.cos(ang), jnp.sin(ang)
    reps = LANES // (2 * half)
    cos_t = jnp.tile(jnp.concatenate([cos, cos], axis=1), (1, reps))
    sin_t = jnp.tile(jnp.concatenate([-sin, sin], axis=1), (1, reps))
    return cos_t, sin_t


def _sigmoid(x):
    return 1.0 / (1.0 + jnp.exp(-x))


def _silu(x):
    return x * _sigmoid(x)


def _inproj_kernel(x_ref, xp_ref, xn_ref, nw_ref, w_ref, cos_ref, sin_ref, alog_ref, dtb_ref,
                   cw_ref, qkv_ref, z_ref, beta_ref, g_ref, dq_ref, dk_ref, dv_ref,
                   sq_ref, sk_ref, sv_ref, *, nseq):
    def normed(x):
        ms = jnp.mean(x * x, axis=-1, keepdims=True)
        return (x * lax.rsqrt(ms + EPS) * nw_ref[...]).astype(BF16)

    hn = normed(x_ref[...])
    cos = cos_ref[...]
    sin = sin_ref[...]
    lane = lax.broadcasted_iota(jnp.int32, cos.shape, 1)
    low = (lane % DIFF_HEAD_DIM) < (DIFF_HEAD_DIM // 2)

    def proj(sec):
        return jnp.dot(hn, w_ref[:, _SEC_OFFS[sec]:_SEC_OFFS[sec + 1]],
                       preferred_element_type=F32)

    def rope_store(sec, out_ref, scale):
        y_all = proj(sec)
        for c0 in range(0, _SEC_SIZES[sec], LANES):
            y = y_all[:, c0:c0 + LANES]
            up = pltpu.roll(y, DIFF_HEAD_DIM // 2, axis=1)
            down = pltpu.roll(y, LANES - DIFF_HEAD_DIM // 2, axis=1)
            r = y * cos + jnp.where(low, down, up) * sin
            if scale != 1.0:
                r = r * scale
            out_ref[:, c0:c0 + LANES] = r.astype(out_ref.dtype)

    tm = hn.shape[0]
    w = DN_WIDTH
    tile = pl.program_id(0) % nseq
    halo = jnp.dot(normed(jnp.concatenate([xp_ref[...], xn_ref[...]], axis=0)),
                   w_ref[:, 0:3 * w], preferred_element_type=F32)
    ext = jnp.concatenate([jnp.where(tile > 0, halo[:CONV_HALO], 0.0), proj(0),
                           jnp.where(tile < nseq - 1, halo[CONV_HALO:], 0.0)], axis=0)
    n_ext = tm + 2 * CONV_HALO
    cw = cw_ref[...]
    y = jnp.zeros((tm, 3 * w), F32)
    for tap in range(DN_CONV):
        off = tap - DN_CONV // 2
        sh = ext if off == 0 else pltpu.roll(ext, (-off) % n_ext, axis=0)
        y = y + sh[CONV_HALO:CONV_HALO + tm] * cw[tap:tap + 1, :]
    y = _silu(y)
    r_i = lax.broadcasted_iota(jnp.int32, (w, w), 0)
    l_i = lax.broadcasted_iota(jnp.int32, (w, w), 1)
    ones_bd = ((r_i // DN_HEAD_DIM) == (l_i // DN_HEAD_DIM)).astype(BF16)

    def seg_sum(t):
        hi = t.astype(BF16)
        lo = (t - hi.astype(F32)).astype(BF16)
        return (jnp.dot(hi, ones_bd, preferred_element_type=F32)
                + jnp.dot(lo, ones_bd, preferred_element_type=F32))

    q, k = y[:, :w], y[:, w:2 * w]
    q = q * lax.rsqrt(seg_sum(q * q) + EPS) * (DN_HEAD_DIM ** -0.5)
    k = k * lax.rsqrt(seg_sum(k * k) + EPS)
    qkv_ref[:, 0:w] = q.astype(qkv_ref.dtype)
    qkv_ref[:, w:2 * w] = k.astype(qkv_ref.dtype)
    qkv_ref[:, 2 * w:] = y[:, 2 * w:].astype(qkv_ref.dtype)
    z_ref[...] = proj(1).astype(z_ref.dtype)
    beta_ref[...] = _sigmoid(proj(2))
    a = proj(3) + dtb_ref[...]
    softplus = jnp.maximum(a, 0.0) + jnp.log(1.0 + jnp.exp(-jnp.abs(a)))
    g_ref[...] = -jnp.exp(alog_ref[...]) * softplus
    rope_store(4, dq_ref, DIFF_HEAD_DIM ** -0.5 * LOG2E)
    rope_store(5, dk_ref, 1.0)
    dv = proj(6)
    hw = 2 * DIFF_HEAD_DIM
    for h in range(DIFF_HEADS):
        dv_ref[0, h] = dv[:, h * hw:(h + 1) * hw].T.astype(dv_ref.dtype)
    rope_store(7, sq_ref, SWA_HEAD_DIM ** -0.5)
    rope_store(8, sk_ref, 1.0)
    sv_ref[...] = proj(9).astype(sv_ref.dtype)


def _inproj(x2d, seq, nw, w_r, cos_t, sin_t, alog_e, dtb_e, cw, tm):
    t, d = x2d.shape
    nseq = seq // tm
    row = lambda i: (i, 0)
    const = lambda i: (0, 0)
    widths = (3 * DN_WIDTH, DN_WIDTH, 2 * DN_WIDTH, 2 * DN_WIDTH, DIFF_WIDTH, DIFF_WIDTH,
              DIFF_WIDTH, SWA_WIDTH, SWA_WIDTH, SWA_WIDTH)
    dtypes = (BF16, BF16, F32, F32, BF16, BF16, BF16, BF16, BF16, BF16)
    dvt_pos = 6
    hw = 2 * DIFF_HEAD_DIM
    dvt_spec = pl.BlockSpec((1, DIFF_HEADS, hw, tm), lambda i: (i // nseq, 0, 0, i % nseq))
    dvt_shape = jax.ShapeDtypeStruct((t // seq, DIFF_HEADS, hw, seq), BF16)
    hpt = tm // CONV_HALO
    last = t // CONV_HALO - 1
    return pl.pallas_call(
        functools.partial(_inproj_kernel, nseq=nseq),
        grid=(t // tm,),
        in_specs=[
            pl.BlockSpec((tm, d), row),
            pl.BlockSpec((CONV_HALO, d), lambda i: (jnp.maximum(i * hpt - 1, 0), 0)),
            pl.BlockSpec((CONV_HALO, d), lambda i: (jnp.minimum((i + 1) * hpt, last), 0)),
            pl.BlockSpec((1, d), const),
            pl.BlockSpec((d, IN_COLS_R), const, pipeline_mode=pl.Buffered(1)),
            pl.BlockSpec((tm, LANES), lambda i: (i % nseq, 0)),
            pl.BlockSpec((tm, LANES), lambda i: (i % nseq, 0)),
            pl.BlockSpec((1, 2 * DN_WIDTH), const),
            pl.BlockSpec((1, 2 * DN_WIDTH), const),
            pl.BlockSpec((8, 3 * DN_WIDTH), const),
        ],
        out_specs=[dvt_spec if k == dvt_pos else pl.BlockSpec((tm, w), row)
                   for k, w in enumerate(widths)],
        out_shape=[dvt_shape if k == dvt_pos else jax.ShapeDtypeStruct((t, w), dt)
                   for k, (w, dt) in enumerate(zip(widths, dtypes))],
        compiler_params=pltpu.CompilerParams(
            dimension_semantics=("parallel",), vmem_limit_bytes=VMEM_LIMIT),
        name="inproj",
    )(x2d, x2d, x2d, nw, w_r, cos_t, sin_t, alog_e, dtb_e, cw)


def _block_diag(m, bd_mask):
    return jnp.where(bd_mask, jnp.concatenate([m] * DN_HEADS, axis=0), jnp.zeros((), m.dtype))


def _dn_block_prep(reverse, x_ref, b_ref, g_ref, tb):
    c = DN_CHUNK
    w = DN_WIDTH
    y = x_ref[0].astype(F32)
    q, k, v = y[:, :w], y[:, w:2 * w], y[:, 2 * w:]
    g = g_ref[0]

    i_c = lax.broadcasted_iota(jnp.int32, (c, w), 0)
    j_c = lax.broadcasted_iota(jnp.int32, (c, w), 1) % DN_HEAD_DIM
    if reverse:
        tri, strict, row_tri = i_c <= j_c, i_c < j_c, i_c >= j_c
    else:
        tri, strict, row_tri = i_c >= j_c, i_c > j_c, i_c <= j_c
    lvl_masks = []
    for lvl in range(int(math.log2(c))):
        siblings = (i_c >> (lvl + 1)) == (j_c >> (lvl + 1))
        odd_i = ((i_c >> lvl) & 1) == 1
        odd_j = ((j_c >> lvl) & 1) == 1
        lvl_masks.append(siblings & ((~odd_i & odd_j) if reverse else (odd_i & ~odd_j)))

    r_b = lax.broadcasted_iota(jnp.int32, (tb, tb), 0)
    c_b = lax.broadcasted_iota(jnp.int32, (tb, tb), 1)
    same = (r_b // c) == (c_b // c)
    cum = (same & ((r_b <= c_b) if reverse else (r_b >= c_b))).astype(F32)
    gc_all = jnp.dot(cum, g, preferred_element_type=F32, precision=lax.Precision.HIGHEST)
    return dict(q=q, k=k, v=v, beta=b_ref[0], g=g, gc=gc_all, tri=tri, strict=strict,
                row_tri=row_tri, eye=(i_c == j_c).astype(F32), lvl_masks=lvl_masks)


def _deltanet_kernel(xf_ref, xb_ref, bf_ref, gf_ref, bb_ref, gb_ref,
                     of_ref, ob_ref, sf_ref, sb_ref, *, tb):
    n = pl.program_id(1)
    c = DN_CHUNK
    w = DN_WIDTH
    nch = tb // c

    @pl.when(n == 0)
    def _():
        sf_ref[...] = jnp.zeros_like(sf_ref)
        sb_ref[...] = jnp.zeros_like(sb_ref)

    r_i = lax.broadcasted_iota(jnp.int32, (w, w), 0)
    l_i = lax.broadcasted_iota(jnp.int32, (w, w), 1)
    bd_mask = (r_i // DN_HEAD_DIM) == (l_i // DN_HEAD_DIM)
    bd2_mask = jnp.concatenate([bd_mask, bd_mask], axis=1)

    def bd2(a, b):
        return jnp.concatenate([_block_diag(a, bd_mask), _block_diag(b, bd_mask)], axis=1)

    prep = (_dn_block_prep(False, xf_ref, bf_ref, gf_ref, tb),
            _dn_block_prep(True, xb_ref, bb_ref, gb_ref, tb))
    units = [(d, t if d == 0 else nch - 1 - t) for t in range(nch) for d in range(2)]

    st = []
    for d, ci in units:
        p = prep[d]
        rows = slice(ci * c, (ci + 1) * c)
        qc, kc, vc, bc, gc = p["q"][rows], p["k"][rows], p["v"][rows], p["beta"][rows], p["gc"][rows]
        gc_row = jnp.sum(jnp.where(p["row_tri"], p["g"][rows], 0.0), axis=0, keepdims=True)
        decay = jnp.where(p["tri"], jnp.exp(jnp.where(p["tri"], gc - gc_row, 0.0)), 0.0)
        g_end = gc[0:1] if d == 1 else gc[c - 1:c]
        eg = jnp.exp(gc)
        kb = kc.astype(BF16)
        gram = lax.dot_general(jnp.concatenate([qc.astype(BF16), kb], axis=0),
                               _block_diag(kb, bd_mask),
                               (((1,), (1,)), ((), ())), preferred_element_type=F32)
        lmat = jnp.where(p["strict"], gram[c:] * bc * decay, 0.0)
        st.append(dict(a_intra=(gram[:c] * decay).astype(BF16), lmat=lmat,
                       tinv=p["eye"] - jnp.where(p["lvl_masks"][0], lmat, 0.0),
                       rhs_u=(vc * bc).astype(BF16), rhs_w=(kc * bc * eg).astype(BF16),
                       q_dec=qc * eg, k_dec=(kc * jnp.exp(g_end - gc)).astype(BF16),
                       gamma=jnp.exp(g_end)))

    for lvl in range(1, int(math.log2(c))):
        m1 = []
        for (d, ci), u in zip(units, st):
            loff = jnp.where(prep[d]["lvl_masks"][lvl], u["lmat"], 0.0).astype(BF16)
            m1.append(jnp.dot(loff, _block_diag(u["tinv"].astype(BF16), bd_mask),
                              preferred_element_type=F32))
        for u, m in zip(st, m1):
            u["tinv"] = u["tinv"] - jnp.dot(u["tinv"].astype(BF16),
                                            _block_diag(m.astype(BF16), bd_mask),
                                            preferred_element_type=F32)

    for u in st:
        uw = jnp.dot(u["tinv"].astype(BF16), bd2(u["rhs_u"], u["rhs_w"]),
                     preferred_element_type=F32)
        u["uw"] = uw.astype(BF16)
    for u in st:
        uwb = u["uw"]
        kuw = lax.dot_general(u["k_dec"], uwb, (((0,), (0,)), ((), ())),
                              preferred_element_type=F32)
        kuw = jnp.where(bd2_mask, kuw, 0.0)
        auw = jnp.dot(u["a_intra"], bd2(uwb[:, :w], uwb[:, w:]),
                      preferred_element_type=F32)
        u["b_bd"] = kuw[:, :w]
        u["lhs"] = jnp.concatenate([kuw[:, w:].astype(BF16),
                                    (u["q_dec"] - auw[:, w:]).astype(BF16)], axis=0)
        u["e"] = auw[:, :w]

    s_refs = (sf_ref, sb_ref)
    o_refs = (of_ref, ob_ref)
    state = [sf_ref[...], sb_ref[...]]
    for (d, ci), u in zip(units, st):
        r = jnp.dot(u["lhs"], state[d].astype(BF16), preferred_element_type=F32)
        o_refs[d][0, ci * c:(ci + 1) * c, :] = r[w:] + u["e"]
        state[d] = state[d] * u["gamma"] - r[:w] + u["b_bd"]
    for d in range(2):
        s_refs[d][...] = state[d]


def _deltanet(qkv, beta, g, tb):
    b, s, _ = qkv.shape
    nb = s // tb
    w3 = 3 * DN_WIDTH
    fwd = lambda bi, n: (bi, n, 0)
    bwd = lambda bi, n: (bi, nb - 1 - n, 0)
    bwd1 = lambda bi, n: (bi, nb - 1 - n, 1)
    gate = lambda im: pl.BlockSpec((1, tb, DN_WIDTH), im)
    return pl.pallas_call(
        functools.partial(_deltanet_kernel, tb=tb),
        grid=(b, nb),
        in_specs=[
            pl.BlockSpec((1, tb, w3), fwd), pl.BlockSpec((1, tb, w3), bwd),
            gate(fwd), gate(fwd), gate(bwd1), gate(bwd1),
        ],
        out_specs=[pl.BlockSpec((1, tb, DN_WIDTH), fwd), pl.BlockSpec((1, tb, DN_WIDTH), bwd)],
        out_shape=[jax.ShapeDtypeStruct((b, s, DN_WIDTH), F32)] * 2,
        scratch_shapes=[pltpu.VMEM((DN_WIDTH, DN_WIDTH), F32)] * 2,
        compiler_params=pltpu.CompilerParams(
            dimension_semantics=("parallel", "arbitrary"), vmem_limit_bytes=VMEM_LIMIT),
        name="deltanet",
    )(qkv, qkv, beta, g, beta, g)


def _diffattn_kernel(q_ref, k_ref, vt_ref, lam_ref, nw_ref, o_ref, *, lam_init, tks):
    d = DIFF_HEAD_DIM
    dv = 2 * d
    q = q_ref[0]
    tq = q.shape[0]
    lane = lax.broadcasted_iota(jnp.int32, q.shape, 1)
    zero = jnp.zeros((), q.dtype)
    q_half = (jnp.where(lane < d, q, zero), jnp.where(lane >= d, q, zero))

    tk = k_ref.shape[1]
    ones = jnp.ones((ONES_ROWS, tks), BF16)

    def scores(sub, j):
        kb = k_ref[0, sub * tks:(sub + 1) * tks, :]
        return lax.dot_general(kb, q_half[j], (((1,), (1,)), ((), ())),
                               preferred_element_type=F32)

    chains = [(sub, j) for sub in range(tk // tks) for j in range(2)]
    m = [jnp.full((1, tq), -jnp.inf, F32)] * 2
    acc = [jnp.zeros((dv + ONES_ROWS, tq), F32)] * 2
    s_next = scores(*chains[0])
    for idx, (sub, j) in enumerate(chains):
        s = s_next
        if idx + 1 < len(chains):
            s_next = scores(*chains[idx + 1])
        vt = jnp.concatenate([vt_ref[0, 0, :, sub * tks:(sub + 1) * tks], ones], axis=0)
        m_new = jnp.maximum(m[j], jnp.max(s, axis=0, keepdims=True))
        alpha = jnp.exp2(m[j] - m_new)
        p = jnp.exp2(s - m_new).astype(BF16)
        acc[j] = alpha * acc[j] + jnp.dot(vt, p, preferred_element_type=F32)
        m[j] = m_new

    lv = lam_ref[...]
    lam = (jnp.exp(jnp.sum(lv[0:1] * lv[1:2])) - jnp.exp(jnp.sum(lv[2:3] * lv[3:4]))
           + lam_init)
    a0, a1 = acc
    o = a0[:dv] / a0[dv:dv + 1] - lam * (a1[:dv] / a1[dv:dv + 1])
    ms = jnp.mean(o * o, axis=0, keepdims=True)
    o = o * lax.rsqrt(ms + EPS) * nw_ref[...] * (1.0 - lam_init)
    o_ref[0] = o.T.astype(o_ref.dtype)


def _diffattn(dq, dk, dvt, lam_p, nw_col, lam_init, tq, tks):
    b, s, _ = dq.shape
    hw = 2 * DIFF_HEAD_DIM
    return pl.pallas_call(
        functools.partial(_diffattn_kernel, lam_init=lam_init, tks=tks),
        grid=(b, DIFF_HEADS, s // tq),
        in_specs=[
            pl.BlockSpec((1, tq, hw), lambda bi, h, qi: (bi, qi, h)),
            pl.BlockSpec((1, s, hw), lambda bi, h, qi: (bi, 0, h)),
            pl.BlockSpec((1, 1, hw, s), lambda bi, h, qi: (bi, h, 0, 0)),
            pl.BlockSpec((4, DIFF_HEAD_DIM), lambda bi, h, qi: (0, 0)),
            pl.BlockSpec((hw, 1), lambda bi, h, qi: (0, 0)),
        ],
        out_specs=pl.BlockSpec((1, tq, hw), lambda bi, h, qi: (bi, qi, h)),
        out_shape=jax.ShapeDtypeStruct((b, s, DIFF_WIDTH), BF16),
        compiler_params=pltpu.CompilerParams(
            dimension_semantics=("parallel", "parallel", "parallel"),
            vmem_limit_bytes=VMEM_LIMIT),
        name="diffattn",
    )(dq, dk, dvt, lam_p, nw_col)


def _swa_kernel(q_ref, kp_ref, km_ref, kn_ref, vp_ref, vm_ref, vn_ref, sink_ref, o_ref, *, tq):
    qi = pl.program_id(1)
    g = pl.program_id(2)
    nq = pl.num_programs(1)
    d = SWA_HEAD_DIM
    grp = SWA_HEADS // SWA_KV_HEADS
    q = q_ref[0]
    k = jnp.concatenate([kp_ref[0], km_ref[0], kn_ref[0]], axis=0)
    v = jnp.concatenate([vp_ref[0], vm_ref[0], vn_ref[0]], axis=0)
    nk = tq + 2 * WINDOW
    qpos = lax.broadcasted_iota(jnp.int32, (tq, nk), 0)
    koff = lax.broadcasted_iota(jnp.int32, (tq, nk), 1) - WINDOW
    valid = jnp.abs(koff - qpos) <= WINDOW
    valid &= (koff >= 0) | (qi > 0)
    valid &= (koff < tq) | (qi < nq - 1)
    lane = lax.broadcasted_iota(jnp.int32, q.shape, 1)
    zero = jnp.zeros((), q.dtype)
    out = jnp.zeros((tq, grp * d), F32)
    for j in range(grp):
        in_head = (lane >= j * d) & (lane < (j + 1) * d)
        s = lax.dot_general(jnp.where(in_head, q, zero), k, (((1,), (1,)), ((), ())),
                            preferred_element_type=F32)
        s = jnp.where(valid, s, -jnp.inf)
        sink = sink_ref[g * grp + j]
        m = jnp.maximum(jnp.max(s, axis=-1, keepdims=True), sink)
        p = jnp.exp(s - m)
        denom = jnp.sum(p, axis=-1, keepdims=True) + jnp.exp(sink - m)
        pv = jnp.dot(p.astype(BF16), v, preferred_element_type=F32)
        out = jnp.where(in_head, pv / denom, out)
    o_ref[0] = out.astype(o_ref.dtype)


def _swa(sq, sk, sv, sink, tq):
    b, s, _ = sq.shape
    nq = s // tq
    wpb = tq // WINDOW
    nw = s // WINDOW
    gw = (SWA_HEADS // SWA_KV_HEADS) * SWA_HEAD_DIM
    main = lambda bi, qi, g: (bi, qi, g)
    prev = lambda bi, qi, g: (bi, jnp.maximum(qi * wpb - 1, 0), g)
    nxt = lambda bi, qi, g: (bi, jnp.minimum((qi + 1) * wpb, nw - 1), g)
    return pl.pallas_call(
        functools.partial(_swa_kernel, tq=tq),
        grid=(b, nq, SWA_KV_HEADS),
        in_specs=[
            pl.BlockSpec((1, tq, gw), main),
            pl.BlockSpec((1, WINDOW, gw), prev), pl.BlockSpec((1, tq, gw), main),
            pl.BlockSpec((1, WINDOW, gw), nxt),
            pl.BlockSpec((1, WINDOW, gw), prev), pl.BlockSpec((1, tq, gw), main),
            pl.BlockSpec((1, WINDOW, gw), nxt),
            pl.BlockSpec(memory_space=pltpu.SMEM),
        ],
        out_specs=pl.BlockSpec((1, tq, gw), main),
        out_shape=jax.ShapeDtypeStruct((b, s, SWA_WIDTH), BF16),
        compiler_params=pltpu.CompilerParams(
            dimension_semantics=("parallel", "parallel", "parallel"),
            vmem_limit_bytes=VMEM_LIMIT),
        name="swa",
    )(sq, sk, sk, sk, sv, sv, sv, sink)


def _outmlp_kernel(x_ref, of_ref, ob_ref, z_ref, df_ref, sw_ref, dnw_ref, wo_ref,
                   n2_ref, wu_ref, wd_ref, fn_ref, y_ref, *, final_norm, ff_chunk, row_groups):
    w = DN_WIDTH
    r_i = lax.broadcasted_iota(jnp.int32, (w, w), 0)
    l_i = lax.broadcasted_iota(jnp.int32, (w, w), 1)
    ones_bd = ((r_i // DN_HEAD_DIM) == (l_i // DN_HEAD_DIM)).astype(BF16)
    tm = x_ref.shape[0]
    groups = [slice(r0, r0 + tm // row_groups) for r0 in range(0, tm, tm // row_groups)]

    def out_proj(rows):
        o = of_ref[rows, :] + ob_ref[rows, :]
        sq = o * o
        hi = sq.astype(BF16)
        lo = (sq - hi.astype(F32)).astype(BF16)
        ss = (jnp.dot(hi, ones_bd, preferred_element_type=F32)
              + jnp.dot(lo, ones_bd, preferred_element_type=F32))
        dn = o * lax.rsqrt(ss * (1.0 / DN_HEAD_DIM) + EPS) * dnw_ref[...]
        dn = dn * _silu(z_ref[rows, :].astype(F32))
        x = x_ref[rows, :]
        x = x + jnp.dot(dn.astype(BF16), wo_ref[0:w, :], preferred_element_type=F32)
        x = x + jnp.dot(df_ref[rows, :], wo_ref[w:w + DIFF_WIDTH, :], preferred_element_type=F32)
        return x + jnp.dot(sw_ref[rows, :], wo_ref[w + DIFF_WIDTH:, :], preferred_element_type=F32)

    def normed(x, nw_ref):
        ms = jnp.mean(x * x, axis=-1, keepdims=True)
        return x * lax.rsqrt(ms + EPS) * nw_ref[...]

    xs = [out_proj(rows) for rows in groups]
    hns = [normed(x, n2_ref).astype(BF16) for x in xs]
    mlps = [jnp.zeros_like(x) for x in xs]
    for c0 in range(0, D_FF, ff_chunk):
        for gi, hn in enumerate(hns):
            h = jnp.dot(hn, wu_ref[:, c0:c0 + ff_chunk], preferred_element_type=F32)
            h = jnp.square(jnp.maximum(h, 0.0)).astype(BF16)
            mlps[gi] = mlps[gi] + jnp.dot(h, wd_ref[c0:c0 + ff_chunk, :],
                                          preferred_element_type=F32)
    for rows, x, mlp in zip(groups, xs, mlps):
        x = x + mlp
        y_ref[rows, :] = normed(x, fn_ref) if final_norm else x


def _outmlp(x2d, o_f, o_b, z, df, sw, dnw_e, wo, n2, wu, wd, fn, final_norm, tm):
    t, d = x2d.shape
    row = lambda i: (i, 0)
    const = lambda i: (0, 0)
    resident = lambda shape: pl.BlockSpec(shape, const)
    return pl.pallas_call(
        functools.partial(_outmlp_kernel, final_norm=final_norm, ff_chunk=1024, row_groups=2),
        grid=(t // tm,),
        in_specs=[
            pl.BlockSpec((tm, d), row),
            pl.BlockSpec((tm, DN_WIDTH), row), pl.BlockSpec((tm, DN_WIDTH), row),
            pl.BlockSpec((tm, DN_WIDTH), row), pl.BlockSpec((tm, DIFF_WIDTH), row),
            pl.BlockSpec((tm, SWA_WIDTH), row),
            pl.BlockSpec((1, DN_WIDTH), const),
            resident((d, d)), pl.BlockSpec((1, d), const),
            resident((d, D_FF)), resident((D_FF, d)),
            pl.BlockSpec((1, d), const),
        ],
        out_specs=pl.BlockSpec((tm, d), row),
        out_shape=jax.ShapeDtypeStruct((t, d), F32),
        compiler_params=pltpu.CompilerParams(
            dimension_semantics=("parallel",), vmem_limit_bytes=VMEM_LIMIT),
        name="outmlp",
    )(x2d, o_f, o_b, z, df, sw, dnw_e, wo, n2, wu, wd, fn)


def _pick(n, pref):
    t = min(n, pref)
    assert n % t == 0, (n, pref)
    return t


def _trunk(x, params):
    b, s, d = x.shape
    assert d == D_MODEL and s % DN_CHUNK == 0 and s % WINDOW == 0
    depth = params["w_in_r"].shape[0]
    tm = _pick(s, 512)
    cos_t, sin_t = _rope_tables(s)
    x2d = x.reshape(b * s, d)
    for li in range(depth):
        p = {k: v[li] for k, v in params.items() if k != "final_norm_w"}
        lam_init = 0.8 - 0.6 * math.exp(-0.3 * li)
        qkv, z, beta, g, dq, dk, dv, sq, sk, sv = _inproj(
            x2d, s, p["norm1_w"], p["w_in_r"], cos_t, sin_t, p["alog_e"], p["dtb_e"],
            p["conv_w"], tm)
        r3 = lambda a: a.reshape(b, s, a.shape[-1])
        o_f, o_b = _deltanet(r3(qkv), r3(beta), r3(g), _pick(s, 256))
        df = _diffattn(r3(dq), r3(dk), dv, p["diff_lambda"], p["diff_norm_w"], lam_init,
                       _pick(s, 1024), _pick(s, 512))
        sw = _swa(r3(sq), r3(sk), r3(sv), p["swa_sink"], _pick(s, 512))
        r2 = lambda a: a.reshape(b * s, a.shape[-1])
        x2d = _outmlp(x2d, r2(o_f), r2(o_b), z, r2(df), r2(sw), p["dn_norm_e"], p["w_out"],
                      p["norm2_w"], p["w_up"], p["w_down"], params["final_norm_w"],
                      li == depth - 1, tm)
    return x2d.reshape(b, s, d)


def kernel(x_prompt, x_sample, norm1_w, w_in, dn_conv_w, dn_a_log, dn_dt_bias, dn_norm_w,
           diff_lambda, diff_norm_w, swa_sink, w_out, norm2_w, w_up, w_down, final_norm_w):
    depth = w_in.shape[0]
    rep = lambda a: jnp.repeat(a.astype(F32).reshape(depth, 1, 2 * DN_HEADS), DN_HEAD_DIM, axis=2)
    params = {
        "norm1_w": norm1_w.astype(F32)[:, None, :],
        "w_in_r": _reorder_w_in(w_in),
        "conv_w": jnp.pad(dn_conv_w.astype(F32), ((0, 0), (0, 8 - DN_CONV), (0, 0))),
        "alog_e": rep(dn_a_log),
        "dtb_e": rep(dn_dt_bias),
        "dn_norm_e": jnp.tile(dn_norm_w.astype(F32), (1, DN_HEADS))[:, None, :],
        "diff_lambda": diff_lambda.astype(F32),
        "diff_norm_w": diff_norm_w.astype(F32)[:, :, None],
        "swa_sink": swa_sink.astype(F32),
        "w_out": w_out.astype(BF16),
        "norm2_w": norm2_w.astype(F32)[:, None, :],
        "w_up": w_up.astype(BF16),
        "w_down": w_down.astype(BF16),
        "final_norm_w": final_norm_w.astype(F32)[None, :],
    }
    return _trunk(x_prompt, params), _trunk(x_sample, params)
```

```python
import functools
import math

import numpy as np
import jax
import jax.numpy as jnp
from jax import lax
from jax.experimental import pallas as pl
from jax.experimental.pallas import tpu as pltpu

F32 = jnp.float32
BF16 = jnp.bfloat16

D_MODEL = 1024
DN_HEADS = 4
DN_HEAD_DIM = 64
DN_WIDTH = DN_HEADS * DN_HEAD_DIM
DN_CONV = 5
DN_CHUNK = 64
DIFF_HEADS = 4
DIFF_HEAD_DIM = 64
DIFF_WIDTH = DIFF_HEADS * 2 * DIFF_HEAD_DIM
SWA_HEADS = 4
SWA_KV_HEADS = 2
SWA_HEAD_DIM = 64
SWA_WIDTH = SWA_HEADS * SWA_HEAD_DIM
WINDOW = 128
ROPE_THETA = 10000.0
D_FF = 4 * D_MODEL
EPS = 1e-6

LANES = 128
CONV_HALO = 8
ONES_ROWS = 16
LOG2E = math.log2(math.e)
VMEM_LIMIT = 56 * 1024 * 1024

_SEC_SIZES = (3 * DN_WIDTH, DN_WIDTH, 2 * DN_WIDTH, 2 * DN_WIDTH,
              DIFF_WIDTH, DIFF_WIDTH, DIFF_WIDTH, SWA_WIDTH, SWA_WIDTH, SWA_WIDTH)
_SEC_OFFS = tuple(int(v) for v in np.cumsum((0,) + _SEC_SIZES))
IN_COLS_R = _SEC_OFFS[-1]


def _reorder_w_in(w_in):
    depth, d, _ = w_in.shape
    w = w_in.astype(BF16)
    o_b = 4 * DN_WIDTH
    o_dq = o_b + 4 * DN_HEADS
    o_sk = o_dq + 3 * DIFF_WIDTH + SWA_WIDTH
    kv_w = SWA_KV_HEADS * SWA_HEAD_DIM
    grp = SWA_HEADS // SWA_KV_HEADS

    def per_group(cols):
        c = cols.reshape(depth, d, SWA_KV_HEADS, 1, SWA_HEAD_DIM)
        return jnp.broadcast_to(c, (depth, d, SWA_KV_HEADS, grp, SWA_HEAD_DIM)).reshape(depth, d, -1)

    out = jnp.concatenate([
        w[:, :, :o_b],
        jnp.repeat(w[:, :, o_b:o_dq], DN_HEAD_DIM, axis=2),
        w[:, :, o_dq:o_sk],
        per_group(w[:, :, o_sk:o_sk + kv_w]),
        per_group(w[:, :, o_sk + kv_w:o_sk + 2 * kv_w])], axis=2)
    assert out.shape[2] == IN_COLS_R
    return out


def _rope_tables(seq):
    half = DIFF_HEAD_DIM // 2
    inv = ROPE_THETA ** (-jnp.arange(half, dtype=F32) / half)
    ang = jnp.arange(seq, dtype=F32)[:, None] * inv[None, :]
    cos, sin = jnp.cos(ang), jnp.sin(ang)
    reps = LANES // (2 * half)
    cos_t = jnp.tile(jnp.concatenate([cos, cos], axis=1), (1, reps))
    sin_t = jnp.tile(jnp.concatenate([-sin, sin], axis=1), (1, reps))
    return cos_t, sin_t


def _sigmoid(x):
    return 1.0 / (1.0 + jnp.exp(-x))


def _silu(x):
    return x * _sigmoid(x)


def _inproj_kernel(x_ref, xp_ref, xn_ref, nw_ref, w_ref, cos_ref, sin_ref, alog_ref, dtb_ref,
                   cw_ref, qkv_ref, z_ref, beta_ref, g_ref, dq_ref, dk_ref, dv_ref,
                   sq_ref, sk_ref, sv_ref, *, nseq):
    def normed(x):
        ms = jnp.mean(x * x, axis=-1, keepdims=True)
        return (x * lax.rsqrt(ms + EPS) * nw_ref[...]).astype(BF16)

    hn = normed(x_ref[...])
    cos = cos_ref[...]
    sin = sin_ref[...]
    lane = lax.broadcasted_iota(jnp.int32, cos.shape, 1)
    low = (lane % DIFF_HEAD_DIM) < (DIFF_HEAD_DIM // 2)

    def proj(sec):
        return jnp.dot(hn, w_ref[:, _SEC_OFFS[sec]:_SEC_OFFS[sec + 1]],
                       preferred_element_type=F32)

    def rope_store(sec, out_ref, scale):
        y_all = proj(sec)
        for c0 in range(0, _SEC_SIZES[sec], LANES):
            y = y_all[:, c0:c0 + LANES]
            up = pltpu.roll(y, DIFF_HEAD_DIM // 2, axis=1)
            down = pltpu.roll(y, LANES - DIFF_HEAD_DIM // 2, axis=1)
            r = y * cos + jnp.where(low, down, up) * sin
            if scale != 1.0:
                r = r * scale
            out_ref[:, c0:c0 + LANES] = r.astype(out_ref.dtype)

    tm = hn.shape[0]
    w = DN_WIDTH
    tile = pl.program_id(0) % nseq
    halo = jnp.dot(normed(jnp.concatenate([xp_ref[...], xn_ref[...]], axis=0)),
                   w_ref[:, 0:3 * w], preferred_element_type=F32)
    ext = jnp.concatenate([jnp.where(tile > 0, halo[:CONV_HALO], 0.0), proj(0),
                           jnp.where(tile < nseq - 1, halo[CONV_HALO:], 0.0)], axis=0)
    z_ref[...] = proj(1).astype(z_ref.dtype)
    beta_ref[...] = _sigmoid(proj(2))
    a = proj(3) + dtb_ref[...]
    softplus = jnp.maximum(a, 0.0) + jnp.log(1.0 + jnp.exp(-jnp.abs(a)))
    g_ref[...] = -jnp.exp(alog_ref[...]) * softplus
    rope_store(4, dq_ref, DIFF_HEAD_DIM ** -0.5 * LOG2E)
    rope_store(5, dk_ref, 1.0)
    dv = proj(6)
    hw = 2 * DIFF_HEAD_DIM
    for h in range(DIFF_HEADS):
        dv_ref[0, h] = dv[:, h * hw:(h + 1) * hw].T.astype(dv_ref.dtype)
    rope_store(7, sq_ref, SWA_HEAD_DIM ** -0.5)
    rope_store(8, sk_ref, 1.0)
    sv_ref[...] = proj(9).astype(sv_ref.dtype)

    n_ext = tm + 2 * CONV_HALO
    cw = cw_ref[...]
    y = jnp.zeros((tm, 3 * w), F32)
    for tap in range(DN_CONV):
        off = tap - DN_CONV // 2
        sh = ext if off == 0 else pltpu.roll(ext, (-off) % n_ext, axis=0)
        y = y + sh[CONV_HALO:CONV_HALO + tm] * cw[tap:tap + 1, :]
    y = _silu(y)
    r_i = lax.broadcasted_iota(jnp.int32, (w, w), 0)
    l_i = lax.broadcasted_iota(jnp.int32, (w, w), 1)
    ones_bd = ((r_i // DN_HEAD_DIM) == (l_i // DN_HEAD_DIM)).astype(BF16)

    def seg_sum(t):
        hi = t.astype(BF16)
        lo = (t - hi.astype(F32)).astype(BF16)
        return (jnp.dot(hi, ones_bd, preferred_element_type=F32)
                + jnp.dot(lo, ones_bd, preferred_element_type=F32))

    q, k = y[:, :w], y[:, w:2 * w]
    q = q * lax.rsqrt(seg_sum(q * q) + EPS) * (DN_HEAD_DIM ** -0.5)
    k = k * lax.rsqrt(seg_sum(k * k) + EPS)
    qkv_ref[:, 0:w] = q.astype(qkv_ref.dtype)
    qkv_ref[:, w:2 * w] = k.astype(qkv_ref.dtype)
    qkv_ref[:, 2 * w:] = y[:, 2 * w:].astype(qkv_ref.dtype)


def _inproj(x2d, seq, nw, w_r, cos_t, sin_t, alog_e, dtb_e, cw, tm):
    t, d = x2d.shape
    nseq = seq // tm
    row = lambda i: (i, 0)
    const = lambda i: (0, 0)
    widths = (3 * DN_WIDTH, DN_WIDTH, 2 * DN_WIDTH, 2 * DN_WIDTH, DIFF_WIDTH, DIFF_WIDTH,
              DIFF_WIDTH, SWA_WIDTH, SWA_WIDTH, SWA_WIDTH)
    dtypes = (BF16, BF16, F32, F32, BF16, BF16, BF16, BF16, BF16, BF16)
    dvt_pos = 6
    hw = 2 * DIFF_HEAD_DIM
    dvt_spec = pl.BlockSpec((1, DIFF_HEADS, hw, tm), lambda i: (i // nseq, 0, 0, i % nseq))
    dvt_shape = jax.ShapeDtypeStruct((t // seq, DIFF_HEADS, hw, seq), BF16)
    hpt = tm // CONV_HALO
    last = t // CONV_HALO - 1
    return pl.pallas_call(
        functools.partial(_inproj_kernel, nseq=nseq),
        grid=(t // tm,),
        in_specs=[
            pl.BlockSpec((tm, d), row),
            pl.BlockSpec((CONV_HALO, d), lambda i: (jnp.maximum(i * hpt - 1, 0), 0)),
            pl.BlockSpec((CONV_HALO, d), lambda i: (jnp.minimum((i + 1) * hpt, last), 0)),
            pl.BlockSpec((1, d), const),
            pl.BlockSpec((d, IN_COLS_R), const, pipeline_mode=pl.Buffered(1)),
            pl.BlockSpec((tm, LANES), lambda i: (i % nseq, 0)),
            pl.BlockSpec((tm, LANES), lambda i: (i % nseq, 0)),
            pl.BlockSpec((1, 2 * DN_WIDTH), const),
            pl.BlockSpec((1, 2 * DN_WIDTH), const),
            pl.BlockSpec((8, 3 * DN_WIDTH), const),
        ],
        out_specs=[dvt_spec if k == dvt_pos else pl.BlockSpec((tm, w), row)
                   for k, w in enumerate(widths)],
        out_shape=[dvt_shape if k == dvt_pos else jax.ShapeDtypeStruct((t, w), dt)
                   for k, (w, dt) in enumerate(zip(widths, dtypes))],
        compiler_params=pltpu.CompilerParams(
            dimension_semantics=("parallel",), vmem_limit_bytes=VMEM_LIMIT),
        name="inproj",
    )(x2d, x2d, x2d, nw, w_r, cos_t, sin_t, alog_e, dtb_e, cw)


def _block_diag(m, bd_mask):
    return jnp.where(bd_mask, jnp.concatenate([m] * DN_HEADS, axis=0), jnp.zeros((), m.dtype))


def _dn_block_prep(reverse, x_ref, b_ref, g_ref, tb):
    c = DN_CHUNK
    w = DN_WIDTH
    y = x_ref[0].astype(F32)
    q, k, v = y[:, :w], y[:, w:2 * w], y[:, 2 * w:]
    g = g_ref[0]

    i_c = lax.broadcasted_iota(jnp.int32, (c, w), 0)
    j_c = lax.broadcasted_iota(jnp.int32, (c, w), 1) % DN_HEAD_DIM
    if reverse:
        tri, strict, row_tri = i_c <= j_c, i_c < j_c, i_c >= j_c
    else:
        tri, strict, row_tri = i_c >= j_c, i_c > j_c, i_c <= j_c
    lvl_masks = []
    for lvl in range(int(math.log2(c))):
        siblings = (i_c >> (lvl + 1)) == (j_c >> (lvl + 1))
        odd_i = ((i_c >> lvl) & 1) == 1
        odd_j = ((j_c >> lvl) & 1) == 1
        lvl_masks.append(siblings & ((~odd_i & odd_j) if reverse else (odd_i & ~odd_j)))

    r_b = lax.broadcasted_iota(jnp.int32, (tb, tb), 0)
    c_b = lax.broadcasted_iota(jnp.int32, (tb, tb), 1)
    same = (r_b // c) == (c_b // c)
    cum = (same & ((r_b <= c_b) if reverse else (r_b >= c_b))).astype(F32)
    gc_all = jnp.dot(cum, g, preferred_element_type=F32, precision=lax.Precision.HIGHEST)
    return dict(q=q, k=k, v=v, beta=b_ref[0], g=g, gc=gc_all, tri=tri, strict=strict,
                row_tri=row_tri, eye=(i_c == j_c).astype(F32), lvl_masks=lvl_masks)


def _deltanet_kernel(xf_ref, xb_ref, bf_ref, gf_ref, bb_ref, gb_ref,
                     of_ref, ob_ref, sf_ref, sb_ref, *, tb):
    n = pl.program_id(1)
    c = DN_CHUNK
    w = DN_WIDTH
    nch = tb // c

    @pl.when(n == 0)
    def _():
        sf_ref[...] = jnp.zeros_like(sf_ref)
        sb_ref[...] = jnp.zeros_like(sb_ref)

    r_i = lax.broadcasted_iota(jnp.int32, (w, w), 0)
    l_i = lax.broadcasted_iota(jnp.int32, (w, w), 1)
    bd_mask = (r_i // DN_HEAD_DIM) == (l_i // DN_HEAD_DIM)
    bd2_mask = jnp.concatenate([bd_mask, bd_mask], axis=1)

    def bd2(a, b):
        return jnp.concatenate([_block_diag(a, bd_mask), _block_diag(b, bd_mask)], axis=1)

    prep = (_dn_block_prep(False, xf_ref, bf_ref, gf_ref, tb),
            _dn_block_prep(True, xb_ref, bb_ref, gb_ref, tb))
    units = [(d, t if d == 0 else nch - 1 - t) for t in range(nch) for d in range(2)]

    st = []
    for d, ci in units:
        p = prep[d]
        rows = slice(ci * c, (ci + 1) * c)
        qc, kc, vc, bc, gc = p["q"][rows], p["k"][rows], p["v"][rows], p["beta"][rows], p["gc"][rows]
        gc_row = jnp.sum(jnp.where(p["row_tri"], p["g"][rows], 0.0), axis=0, keepdims=True)
        decay = jnp.where(p["tri"], jnp.exp(jnp.where(p["tri"], gc - gc_row, 0.0)), 0.0)
        g_end = gc[0:1] if d == 1 else gc[c - 1:c]
        eg = jnp.exp(gc)
        kb = kc.astype(BF16)
        gram = lax.dot_general(jnp.concatenate([qc.astype(BF16), kb], axis=0),
                               _block_diag(kb, bd_mask),
                               (((1,), (1,)), ((), ())), preferred_element_type=F32)
        lmat = jnp.where(p["strict"], gram[c:] * bc * decay, 0.0)
        st.append(dict(a_intra=(gram[:c] * decay).astype(BF16), lmat=lmat,
                       tinv=p["eye"] - jnp.where(p["lvl_masks"][0], lmat, 0.0),
                       rhs_u=(vc * bc).astype(BF16), rhs_w=(kc * bc * eg).astype(BF16),
                       q_dec=qc * eg, k_dec=(kc * jnp.exp(g_end - gc)).astype(BF16),
                       gamma=jnp.exp(g_end)))

    for lvl in range(1, int(math.log2(c))):
        m1 = []
        for (d, ci), u in zip(units, st):
            loff = jnp.where(prep[d]["lvl_masks"][lvl], u["lmat"], 0.0).astype(BF16)
            m1.append(jnp.dot(loff, _block_diag(u["tinv"].astype(BF16), bd_mask),
                              preferred_element_type=F32))
        for u, m in zip(st, m1):
            u["tinv"] = u["tinv"] - jnp.dot(u["tinv"].astype(BF16),
                                            _block_diag(m.astype(BF16), bd_mask),
                                            preferred_element_type=F32)

    for u in st:
        uw = jnp.dot(u["tinv"].astype(BF16), bd2(u["rhs_u"], u["rhs_w"]),
                     preferred_element_type=F32)
        u["uw"] = uw.astype(BF16)
    for u in st:
        uwb = u["uw"]
        kuw = lax.dot_general(u["k_dec"], uwb, (((0,), (0,)), ((), ())),
                              preferred_element_type=F32)
        kuw = jnp.where(bd2_mask, kuw, 0.0)
        auw = jnp.dot(u["a_intra"], bd2(uwb[:, :w], uwb[:, w:]),
                      preferred_element_type=F32)
        u["b_bd"] = kuw[:, :w]
        u["lhs"] = jnp.concatenate([kuw[:, w:].astype(BF16),
                                    (u["q_dec"] - auw[:, w:]).astype(BF16)], axis=0)
        u["e"] = auw[:, :w]

    s_refs = (sf_ref, sb_ref)
    o_refs = (of_ref, ob_ref)
    state = [sf_ref[...], sb_ref[...]]
    for (d, ci), u in zip(units, st):
        r = jnp.dot(u["lhs"], state[d].astype(BF16), preferred_element_type=F32)
        o_refs[d][0, ci * c:(ci + 1) * c, :] = r[w:] + u["e"]
        state[d] = state[d] * u["gamma"] - r[:w] + u["b_bd"]
    for d in range(2):
        s_refs[d][...] = state[d]


def _deltanet(qkv, beta, g, tb):
    b, s, _ = qkv.shape
    nb = s // tb
    w3 = 3 * DN_WIDTH
    fwd = lambda bi, n: (bi, n, 0)
    bwd = lambda bi, n: (bi, nb - 1 - n, 0)
    bwd1 = lambda bi, n: (bi, nb - 1 - n, 1)
    gate = lambda im: pl.BlockSpec((1, tb, DN_WIDTH), im)
    return pl.pallas_call(
        functools.partial(_deltanet_kernel, tb=tb),
        grid=(b, nb),
        in_specs=[
            pl.BlockSpec((1, tb, w3), fwd), pl.BlockSpec((1, tb, w3), bwd),
            gate(fwd), gate(fwd), gate(bwd1), gate(bwd1),
        ],
        out_specs=[pl.BlockSpec((1, tb, DN_WIDTH), fwd), pl.BlockSpec((1, tb, DN_WIDTH), bwd)],
        out_shape=[jax.ShapeDtypeStruct((b, s, DN_WIDTH), F32)] * 2,
        scratch_shapes=[pltpu.VMEM((DN_WIDTH, DN_WIDTH), F32)] * 2,
        compiler_params=pltpu.CompilerParams(
            dimension_semantics=("parallel", "arbitrary"), vmem_limit_bytes=VMEM_LIMIT),
        name="deltanet",
    )(qkv, qkv, beta, g, beta, g)


def _diffattn_kernel(q_ref, k_ref, vt_ref, lam_ref, nw_ref, o_ref, *, lam_init, tks):
    d = DIFF_HEAD_DIM
    dv = 2 * d
    q = q_ref[0]
    tq = q.shape[0]
    lane = lax.broadcasted_iota(jnp.int32, q.shape, 1)
    zero = jnp.zeros((), q.dtype)
    q_half = (jnp.where(lane < d, q, zero), jnp.where(lane >= d, q, zero))

    tk = k_ref.shape[1]
    ones = jnp.ones((ONES_ROWS, tks), BF16)

    def scores(sub, j):
        kb = k_ref[0, sub * tks:(sub + 1) * tks, :]
        return lax.dot_general(kb, q_half[j], (((1,), (1,)), ((), ())),
                               preferred_element_type=F32)

    chains = [(sub, j) for sub in range(tk // tks) for j in range(2)]
    m = [jnp.full((1, tq), -jnp.inf, F32)] * 2
    acc = [jnp.zeros((dv + ONES_ROWS, tq), F32)] * 2
    s_next = scores(*chains[0])
    for idx, (sub, j) in enumerate(chains):
        s = s_next
        if idx + 1 < len(chains):
            s_next = scores(*chains[idx + 1])
        vt = jnp.concatenate([vt_ref[0, 0, :, sub * tks:(sub + 1) * tks], ones], axis=0)
        m_new = jnp.maximum(m[j], jnp.max(s, axis=0, keepdims=True))
        alpha = jnp.exp2(m[j] - m_new)
        p = jnp.exp2(s - m_new).astype(BF16)
        acc[j] = alpha * acc[j] + jnp.dot(vt, p, preferred_element_type=F32)
        m[j] = m_new

    lv = lam_ref[...]
    lam = (jnp.exp(jnp.sum(lv[0:1] * lv[1:2])) - jnp.exp(jnp.sum(lv[2:3] * lv[3:4]))
           + lam_init)
    a0, a1 = acc
    o = a0[:dv] / a0[dv:dv + 1] - lam * (a1[:dv] / a1[dv:dv + 1])
    ms = jnp.mean(o * o, axis=0, keepdims=True)
    o = o * lax.rsqrt(ms + EPS) * nw_ref[...] * (1.0 - lam_init)
    o_ref[0] = o.T.astype(o_ref.dtype)


def _diffattn(dq, dk, dvt, lam_p, nw_col, lam_init, tq, tks):
    b, s, _ = dq.shape
    hw = 2 * DIFF_HEAD_DIM
    return pl.pallas_call(
        functools.partial(_diffattn_kernel, lam_init=lam_init, tks=tks),
        grid=(b, DIFF_HEADS, s // tq),
        in_specs=[
            pl.BlockSpec((1, tq, hw), lambda bi, h, qi: (bi, qi, h)),
            pl.BlockSpec((1, s, hw), lambda bi, h, qi: (bi, 0, h)),
            pl.BlockSpec((1, 1, hw, s), lambda bi, h, qi: (bi, h, 0, 0)),
            pl.BlockSpec((4, DIFF_HEAD_DIM), lambda bi, h, qi: (0, 0)),
            pl.BlockSpec((hw, 1), lambda bi, h, qi: (0, 0)),
        ],
        out_specs=pl.BlockSpec((1, tq, hw), lambda bi, h, qi: (bi, qi, h)),
        out_shape=jax.ShapeDtypeStruct((b, s, DIFF_WIDTH), BF16),
        compiler_params=pltpu.CompilerParams(
            dimension_semantics=("parallel", "parallel", "parallel"),
            vmem_limit_bytes=VMEM_LIMIT),
        name="diffattn",
    )(dq, dk, dvt, lam_p, nw_col)


def _swa_kernel(q_ref, kp_ref, km_ref, kn_ref, vp_ref, vm_ref, vn_ref, sink_ref, o_ref, *, tq):
    qi = pl.program_id(1)
    g = pl.program_id(2)
    nq = pl.num_programs(1)
    d = SWA_HEAD_DIM
    grp = SWA_HEADS // SWA_KV_HEADS
    k = jnp.concatenate([kp_ref[0], km_ref[0], kn_ref[0]], axis=0)
    v = jnp.concatenate([vp_ref[0], vm_ref[0], vn_ref[0]], axis=0)
    nk = 3 * WINDOW
    nblk = tq // WINDOW
    qpos = lax.broadcasted_iota(jnp.int32, (WINDOW, nk), 0)
    koff = lax.broadcasted_iota(jnp.int32, (WINDOW, nk), 1) - WINDOW
    band = jnp.abs(koff - qpos) <= WINDOW
    lane = lax.broadcasted_iota(jnp.int32, (WINDOW, grp * d), 1)
    zero = jnp.zeros((), q_ref.dtype)
    for qb in range(nblk):
        q = q_ref[0, qb * WINDOW:(qb + 1) * WINDOW, :]
        kb = k[qb * WINDOW:qb * WINDOW + nk]
        vb = v[qb * WINDOW:qb * WINDOW + nk]
        valid = band
        if qb == 0:
            valid = valid & ((koff >= 0) | (qi > 0))
        if qb == nblk - 1:
            valid = valid & ((koff < WINDOW) | (qi < nq - 1))
        out = jnp.zeros((WINDOW, grp * d), F32)
        for j in range(grp):
            in_head = (lane >= j * d) & (lane < (j + 1) * d)
            s = lax.dot_general(jnp.where(in_head, q, zero), kb, (((1,), (1,)), ((), ())),
                                preferred_element_type=F32)
            s = jnp.where(valid, s, -jnp.inf)
            sink = sink_ref[g * grp + j]
            m = jnp.maximum(jnp.max(s, axis=-1, keepdims=True), sink)
            p = jnp.exp(s - m)
            denom = jnp.sum(p, axis=-1, keepdims=True) + jnp.exp(sink - m)
            pv = jnp.dot(p.astype(BF16), vb, preferred_element_type=F32)
            out = jnp.where(in_head, pv / denom, out)
        o_ref[0, qb * WINDOW:(qb + 1) * WINDOW, :] = out.astype(o_ref.dtype)


def _swa(sq, sk, sv, sink, tq):
    b, s, _ = sq.shape
    nq = s // tq
    wpb = tq // WINDOW
    nw = s // WINDOW
    gw = (SWA_HEADS // SWA_KV_HEADS) * SWA_HEAD_DIM
    main = lambda bi, qi, g: (bi, qi, g)
    prev = lambda bi, qi, g: (bi, jnp.maximum(qi * wpb - 1, 0), g)
    nxt = lambda bi, qi, g: (bi, jnp.minimum((qi + 1) * wpb, nw - 1), g)
    return pl.pallas_call(
        functools.partial(_swa_kernel, tq=tq),
        grid=(b, nq, SWA_KV_HEADS),
        in_specs=[
            pl.BlockSpec((1, tq, gw), main),
            pl.BlockSpec((1, WINDOW, gw), prev), pl.BlockSpec((1, tq, gw), main),
            pl.BlockSpec((1, WINDOW, gw), nxt),
            pl.BlockSpec((1, WINDOW, gw), prev), pl.BlockSpec((1, tq, gw), main),
            pl.BlockSpec((1, WINDOW, gw), nxt),
            pl.BlockSpec(memory_space=pltpu.SMEM),
        ],
        out_specs=pl.BlockSpec((1, tq, gw), main),
        out_shape=jax.ShapeDtypeStruct((b, s, SWA_WIDTH), BF16),
        compiler_params=pltpu.CompilerParams(
            dimension_semantics=("parallel", "parallel", "parallel"),
            vmem_limit_bytes=VMEM_LIMIT),
        name="swa",
    )(sq, sk, sk, sk, sv, sv, sv, sink)


def _outmlp_kernel(x_ref, of_ref, ob_ref, z_ref, df_ref, sw_ref, dnw_ref, wo_ref,
                   n2_ref, wu_ref, wd_ref, fn_ref, y_ref, *, final_norm, ff_chunk, row_groups):
    w = DN_WIDTH
    r_i = lax.broadcasted_iota(jnp.int32, (w, w), 0)
    l_i = lax.broadcasted_iota(jnp.int32, (w, w), 1)
    ones_bd = ((r_i // DN_HEAD_DIM) == (l_i // DN_HEAD_DIM)).astype(BF16)
    tm = x_ref.shape[0]
    groups = [slice(r0, r0 + tm // row_groups) for r0 in range(0, tm, tm // row_groups)]

    def out_proj(rows):
        o = of_ref[rows, :] + ob_ref[rows, :]
        sq = o * o
        hi = sq.astype(BF16)
        lo = (sq - hi.astype(F32)).astype(BF16)
        ss = (jnp.dot(hi, ones_bd, preferred_element_type=F32)
              + jnp.dot(lo, ones_bd, preferred_element_type=F32))
        dn = o * lax.rsqrt(ss * (1.0 / DN_HEAD_DIM) + EPS) * dnw_ref[...]
        dn = dn * _silu(z_ref[rows, :].astype(F32))
        x = x_ref[rows, :]
        x = x + jnp.dot(dn.astype(BF16), wo_ref[0:w, :], preferred_element_type=F32)
        x = x + jnp.dot(df_ref[rows, :], wo_ref[w:w + DIFF_WIDTH, :], preferred_element_type=F32)
        return x + jnp.dot(sw_ref[rows, :], wo_ref[w + DIFF_WIDTH:, :], preferred_element_type=F32)

    def normed(x, nw_ref):
        ms = jnp.mean(x * x, axis=-1, keepdims=True)
        return x * lax.rsqrt(ms + EPS) * nw_ref[...]

    xs = [out_proj(rows) for rows in groups]
    hns = [normed(x, n2_ref).astype(BF16) for x in xs]
    mlps = [jnp.zeros_like(x) for x in xs]
    for c0 in range(0, D_FF, ff_chunk):
        for gi, hn in enumerate(hns):
            h = jnp.dot(hn, wu_ref[:, c0:c0 + ff_chunk], preferred_element_type=F32)
            h = jnp.square(jnp.maximum(h, 0.0)).astype(BF16)
            mlps[gi] = mlps[gi] + jnp.dot(h, wd_ref[c0:c0 + ff_chunk, :],
                                          preferred_element_type=F32)
    for rows, x, mlp in zip(groups, xs, mlps):
        x = x + mlp
        y_ref[rows, :] = normed(x, fn_ref) if final_norm else x


def _outmlp(x2d, o_f, o_b, z, df, sw, dnw_e, wo, n2, wu, wd, fn, final_norm, tm):
    t, d = x2d.shape
    row = lambda i: (i, 0)
    const = lambda i: (0, 0)
    resident = lambda shape: pl.BlockSpec(shape, const)
    return pl.pallas_call(
        functools.partial(_outmlp_kernel, final_norm=final_norm, ff_chunk=1024, row_groups=2),
        grid=(t // tm,),
        in_specs=[
            pl.BlockSpec((tm, d), row),
            pl.BlockSpec((tm, DN_WIDTH), row), pl.BlockSpec((tm, DN_WIDTH), row),
            pl.BlockSpec((tm, DN_WIDTH), row), pl.BlockSpec((tm, DIFF_WIDTH), row),
            pl.BlockSpec((tm, SWA_WIDTH), row),
            pl.BlockSpec((1, DN_WIDTH), const),
            resident((d, d)), pl.BlockSpec((1, d), const),
            resident((d, D_FF)), resident((D_FF, d)),
            pl.BlockSpec((1, d), const),
        ],
        out_specs=pl.BlockSpec((tm, d), row),
        out_shape=jax.ShapeDtypeStruct((t, d), F32),
        compiler_params=pltpu.CompilerParams(
            dimension_semantics=("parallel",), vmem_limit_bytes=VMEM_LIMIT),
        name="outmlp",
    )(x2d, o_f, o_b, z, df, sw, dnw_e, wo, n2, wu, wd, fn)


def _pick(n, pref):
    t = min(n, pref)
    assert n % t == 0, (n, pref)
    return t


def _trunk(x, params):
    b, s, d = x.shape
    assert d == D_MODEL and s % DN_CHUNK == 0 and s % WINDOW == 0
    depth = params["w_in_r"].shape[0]
    tm = _pick(s, 512)
    cos_t, sin_t = _rope_tables(s)
    x2d = x.reshape(b * s, d)
    for li in range(depth):
        p = {k: v[li] for k, v in params.items() if k != "final_norm_w"}
        lam_init = 0.8 - 0.6 * math.exp(-0.3 * li)
        qkv, z, beta, g, dq, dk, dv, sq, sk, sv = _inproj(
            x2d, s, p["norm1_w"], p["w_in_r"], cos_t, sin_t, p["alog_e"], p["dtb_e"],
            p["conv_w"], tm)
        r3 = lambda a: a.reshape(b, s, a.shape[-1])
        o_f, o_b = _deltanet(r3(qkv), r3(beta), r3(g), _pick(s, 512))
        df = _diffattn(r3(dq), r3(dk), dv, p["diff_lambda"], p["diff_norm_w"], lam_init,
                       _pick(s, 1024), _pick(s, 512))
        sw = _swa(r3(sq), r3(sk), r3(sv), p["swa_sink"], _pick(s, 512))
        r2 = lambda a: a.reshape(b * s, a.shape[-1])
        x2d = _outmlp(x2d, r2(o_f), r2(o_b), z, r2(df), r2(sw), p["dn_norm_e"], p["w_out"],
                      p["norm2_w"], p["w_up"], p["w_down"], params["final_norm_w"],
                      li == depth - 1, tm)
    return x2d.reshape(b, s, d)


def kernel(x_prompt, x_sample, norm1_w, w_in, dn_conv_w, dn_a_log, dn_dt_bias, dn_norm_w,
           diff_lambda, diff_norm_w, swa_sink, w_out, norm2_w, w_up, w_down, final_norm_w):
    depth = w_in.shape[0]
    rep = lambda a: jnp.repeat(a.astype(F32).reshape(depth, 1, 2 * DN_HEADS), DN_HEAD_DIM, axis=2)
    params = {
        "norm1_w": norm1_w.astype(F32)[:, None, :],
        "w_in_r": _reorder_w_in(w_in),
        "conv_w": jnp.pad(dn_conv_w.astype(F32), ((0, 0), (0, 8 - DN_CONV), (0, 0))),
        "alog_e": rep(dn_a_log),
        "dtb_e": rep(dn_dt_bias),
        "dn_norm_e": jnp.tile(dn_norm_w.astype(F32), (1, DN_HEADS))[:, None, :],
        "diff_lambda": diff_lambda.astype(F32),
        "diff_norm_w": diff_norm_w.astype(F32)[:, :, None],
        "swa_sink": swa_sink.astype(F32),
        "w_out": w_out.astype(BF16),
        "norm2_w": norm2_w.astype(F32)[:, None, :],
        "w_up": w_up.astype(BF16),
        "w_down": w_down.astype(BF16),
        "final_norm_w": final_norm_w.astype(F32)[None, :],
    }
    return _trunk(x_prompt, params), _trunk(x_sample, params)
```

```python
import functools
import math

import numpy as np
import jax
import jax.numpy as jnp
from jax import lax
from jax.experimental import pallas as pl
from jax.experimental.pallas import tpu as pltpu

F32 = jnp.float32
BF16 = jnp.bfloat16

D_MODEL = 1024
DN_HEADS = 4
DN_HEAD_DIM = 64
DN_WIDTH = DN_HEADS * DN_HEAD_DIM
DN_CONV = 5
DN_CHUNK = 64
DIFF_HEADS = 4
DIFF_HEAD_DIM = 64
DIFF_WIDTH = DIFF_HEADS * 2 * DIFF_HEAD_DIM
SWA_HEADS = 4
SWA_KV_HEADS = 2
SWA_HEAD_DIM = 64
SWA_WIDTH = SWA_HEADS * SWA_HEAD_DIM
WINDOW = 128
ROPE_THETA = 10000.0
D_FF = 4 * D_MODEL
EPS = 1e-6

LANES = 128
CONV_HALO = 8
ONES_ROWS = 16
LOG2E = math.log2(math.e)
VMEM_LIMIT = 56 * 1024 * 1024

_SEC_SIZES = (3 * DN_WIDTH, DN_WIDTH, 2 * DN_WIDTH, 2 * DN_WIDTH,
              DIFF_WIDTH, DIFF_WIDTH, DIFF_WIDTH, SWA_WIDTH, SWA_WIDTH, SWA_WIDTH)
_SEC_OFFS = tuple(int(v) for v in np.cumsum((0,) + _SEC_SIZES))
IN_COLS_R = _SEC_OFFS[-1]


def _reorder_w_in(w_in):
    depth, d, _ = w_in.shape
    w = w_in.astype(BF16)
    o_b = 4 * DN_WIDTH
    o_dq = o_b + 4 * DN_HEADS
    o_sk = o_dq + 3 * DIFF_WIDTH + SWA_WIDTH
    kv_w = SWA_KV_HEADS * SWA_HEAD_DIM
    grp = SWA_HEADS // SWA_KV_HEADS

    def per_group(cols):
        c = cols.reshape(depth, d, SWA_KV_HEADS, 1, SWA_HEAD_DIM)
        return jnp.broadcast_to(c, (depth, d, SWA_KV_HEADS, grp, SWA_HEAD_DIM)).reshape(depth, d, -1)

    out = jnp.concatenate([
        w[:, :, :o_b],
        jnp.repeat(w[:, :, o_b:o_dq], DN_HEAD_DIM, axis=2),
        w[:, :, o_dq:o_sk],
        per_group(w[:, :, o_sk:o_sk + kv_w]),
        per_group(w[:, :, o_sk + kv_w:o_sk + 2 * kv_w])], axis=2)
    assert out.shape[2] == IN_COLS_R
    return out


def _rope_tables(seq):
    half = DIFF_HEAD_DIM // 2
    inv = ROPE_THETA ** (-jnp.arange(half, dtype=F32) / half)
    ang = jnp.arange(seq, dtype=F32)[:, None] * inv[None, :]
    cos, sin = jnp.cos(ang), jnp.sin(ang)
    reps = LANES // (2 * half)
    cos_t = jnp.tile(jnp.concatenate([cos, cos], axis=1), (1, reps))
    sin_t = jnp.tile(jnp.concatenate([-sin, sin], axis=1), (1, reps))
    return cos_t, sin_t


def _sigmoid(x):
    return 1.0 / (1.0 + jnp.exp(-x))


def _silu(x):
    return x * _sigmoid(x)


def _inproj_kernel(x_ref, xp_ref, xn_ref, nw_ref, w_ref, cos_ref, sin_ref, alog_ref, dtb_ref,
                   cw_ref, qkv_ref, z_ref, beta_ref, g_ref, dq_ref, dk_ref, dv_ref,
                   sq_ref, sk_ref, sv_ref, *, nseq):
    def normed(x):
        ms = jnp.mean(x * x, axis=-1, keepdims=True)
        return (x * lax.rsqrt(ms + EPS) * nw_ref[...]).astype(BF16)

    hn = normed(x_ref[...])
    cos = cos_ref[...]
    sin = sin_ref[...]
    lane = lax.broadcasted_iota(jnp.int32, cos.shape, 1)
    low = (lane % DIFF_HEAD_DIM) < (DIFF_HEAD_DIM // 2)

    def proj(sec):
        return jnp.dot(hn, w_ref[:, _SEC_OFFS[sec]:_SEC_OFFS[sec + 1]],
                       preferred_element_type=F32)

    def rope_store(sec, out_ref, scale):
        y_all = proj(sec)
        for c0 in range(0, _SEC_SIZES[sec], LANES):
            y = y_all[:, c0:c0 + LANES]
            up = pltpu.roll(y, DIFF_HEAD_DIM // 2, axis=1)
            down = pltpu.roll(y, LANES - DIFF_HEAD_DIM // 2, axis=1)
            r = y * cos + jnp.where(low, down, up) * sin
            if scale != 1.0:
                r = r * scale
            out_ref[:, c0:c0 + LANES] = r.astype(out_ref.dtype)

    tm = hn.shape[0]
    w = DN_WIDTH
    tile = pl.program_id(0) % nseq
    halo = jnp.dot(normed(jnp.concatenate([xp_ref[...], xn_ref[...]], axis=0)),
                   w_ref[:, 0:3 * w], preferred_element_type=F32)
    ext = jnp.concatenate([jnp.where(tile > 0, halo[:CONV_HALO], 0.0), proj(0),
                           jnp.where(tile < nseq - 1, halo[CONV_HALO:], 0.0)], axis=0)
    z_ref[...] = proj(1).astype(z_ref.dtype)
    beta_ref[...] = _sigmoid(proj(2))
    a = proj(3) + dtb_ref[...]
    softplus = jnp.maximum(a, 0.0) + jnp.log(1.0 + jnp.exp(-jnp.abs(a)))
    g_ref[...] = -jnp.exp(alog_ref[...]) * softplus
    rope_store(4, dq_ref, DIFF_HEAD_DIM ** -0.5 * LOG2E)
    rope_store(5, dk_ref, 1.0)
    dv = proj(6)
    hw = 2 * DIFF_HEAD_DIM
    for h in range(DIFF_HEADS):
        dv_ref[0, h] = dv[:, h * hw:(h + 1) * hw].T.astype(dv_ref.dtype)
    rope_store(7, sq_ref, SWA_HEAD_DIM ** -0.5)
    rope_store(8, sk_ref, 1.0)
    sv_ref[...] = proj(9).astype(sv_ref.dtype)

    n_ext = tm + 2 * CONV_HALO
    cw = cw_ref[...]
    y = jnp.zeros((tm, 3 * w), F32)
    for tap in range(DN_CONV):
        off = tap - DN_CONV // 2
        sh = ext if off == 0 else pltpu.roll(ext, (-off) % n_ext, axis=0)
        y = y + sh[CONV_HALO:CONV_HALO + tm] * cw[tap:tap + 1, :]
    y = _silu(y)
    r_i = lax.broadcasted_iota(jnp.int32, (w, w), 0)
    l_i = lax.broadcasted_iota(jnp.int32, (w, w), 1)
    ones_bd = ((r_i // DN_HEAD_DIM) == (l_i // DN_HEAD_DIM)).astype(BF16)

    def seg_sum(t):
        hi = t.astype(BF16)
        lo = (t - hi.astype(F32)).astype(BF16)
        return (jnp.dot(hi, ones_bd, preferred_element_type=F32)
                + jnp.dot(lo, ones_bd, preferred_element_type=F32))

    q, k = y[:, :w], y[:, w:2 * w]
    q = q * lax.rsqrt(seg_sum(q * q) + EPS) * (DN_HEAD_DIM ** -0.5)
    k = k * lax.rsqrt(seg_sum(k * k) + EPS)
    qkv_ref[:, 0:w] = q.astype(qkv_ref.dtype)
    qkv_ref[:, w:2 * w] = k.astype(qkv_ref.dtype)
    qkv_ref[:, 2 * w:] = y[:, 2 * w:].astype(qkv_ref.dtype)


def _inproj(x2d, seq, nw, w_r, cos_t, sin_t, alog_e, dtb_e, cw, tm):
    t, d = x2d.shape
    nseq = seq // tm
    row = lambda i: (i, 0)
    const = lambda i: (0, 0)
    widths = (3 * DN_WIDTH, DN_WIDTH, 2 * DN_WIDTH, 2 * DN_WIDTH, DIFF_WIDTH, DIFF_WIDTH,
              DIFF_WIDTH, SWA_WIDTH, SWA_WIDTH, SWA_WIDTH)
    dtypes = (BF16, BF16, F32, F32, BF16, BF16, BF16, BF16, BF16, BF16)
    dvt_pos = 6
    hw = 2 * DIFF_HEAD_DIM
    dvt_spec = pl.BlockSpec((1, DIFF_HEADS, hw, tm), lambda i: (i // nseq, 0, 0, i % nseq))
    dvt_shape = jax.ShapeDtypeStruct((t // seq, DIFF_HEADS, hw, seq), BF16)
    hpt = tm // CONV_HALO
    last = t // CONV_HALO - 1
    return pl.pallas_call(
        functools.partial(_inproj_kernel, nseq=nseq),
        grid=(t // tm,),
        in_specs=[
            pl.BlockSpec((tm, d), row),
            pl.BlockSpec((CONV_HALO, d), lambda i: (jnp.maximum(i * hpt - 1, 0), 0)),
            pl.BlockSpec((CONV_HALO, d), lambda i: (jnp.minimum((i + 1) * hpt, last), 0)),
            pl.BlockSpec((1, d), const),
            pl.BlockSpec((d, IN_COLS_R), const, pipeline_mode=pl.Buffered(1)),
            pl.BlockSpec((tm, LANES), lambda i: (i % nseq, 0)),
            pl.BlockSpec((tm, LANES), lambda i: (i % nseq, 0)),
            pl.BlockSpec((1, 2 * DN_WIDTH), const),
            pl.BlockSpec((1, 2 * DN_WIDTH), const),
            pl.BlockSpec((8, 3 * DN_WIDTH), const),
        ],
        out_specs=[dvt_spec if k == dvt_pos else pl.BlockSpec((tm, w), row)
                   for k, w in enumerate(widths)],
        out_shape=[dvt_shape if k == dvt_pos else jax.ShapeDtypeStruct((t, w), dt)
                   for k, (w, dt) in enumerate(zip(widths, dtypes))],
        compiler_params=pltpu.CompilerParams(
            dimension_semantics=("parallel",), vmem_limit_bytes=VMEM_LIMIT),
        name="inproj",
    )(x2d, x2d, x2d, nw, w_r, cos_t, sin_t, alog_e, dtb_e, cw)


def _block_diag(m, bd_mask):
    return jnp.where(bd_mask, jnp.concatenate([m] * DN_HEADS, axis=0), jnp.zeros((), m.dtype))


def _dn_block_prep(reverse, x_ref, b_ref, g_ref, tb):
    c = DN_CHUNK
    w = DN_WIDTH
    y = x_ref[0].astype(F32)
    q, k, v = y[:, :w], y[:, w:2 * w], y[:, 2 * w:]
    g = g_ref[0]

    i_c = lax.broadcasted_iota(jnp.int32, (c, w), 0)
    j_c = lax.broadcasted_iota(jnp.int32, (c, w), 1) % DN_HEAD_DIM
    if reverse:
        tri, strict, row_tri = i_c <= j_c, i_c < j_c, i_c >= j_c
    else:
        tri, strict, row_tri = i_c >= j_c, i_c > j_c, i_c <= j_c
    lvl_masks = []
    for lvl in range(int(math.log2(c))):
        siblings = (i_c >> (lvl + 1)) == (j_c >> (lvl + 1))
        odd_i = ((i_c >> lvl) & 1) == 1
        odd_j = ((j_c >> lvl) & 1) == 1
        lvl_masks.append(siblings & ((~odd_i & odd_j) if reverse else (odd_i & ~odd_j)))

    r_b = lax.broadcasted_iota(jnp.int32, (c, c), 0)
    c_b = lax.broadcasted_iota(jnp.int32, (c, c), 1)
    cum = ((r_b <= c_b) if reverse else (r_b >= c_b)).astype(BF16)
    g_hi = g.astype(BF16)
    g_r1 = g - g_hi.astype(F32)
    g_mid = g_r1.astype(BF16)
    g_lo = (g_r1 - g_mid.astype(F32)).astype(BF16)
    g3 = jnp.concatenate([g_hi, g_mid, g_lo], axis=1)
    gc_chunks = []
    for ci in range(tb // c):
        t3 = jnp.dot(cum, g3[ci * c:(ci + 1) * c], preferred_element_type=F32)
        gc_chunks.append(t3[:, :w] + t3[:, w:2 * w] + t3[:, 2 * w:])
    gc_all = jnp.concatenate(gc_chunks, axis=0)
    return dict(q=q, k=k, v=v, beta=b_ref[0], g=g, gc=gc_all, tri=tri, strict=strict,
                row_tri=row_tri, eye=(i_c == j_c).astype(F32), lvl_masks=lvl_masks)


def _deltanet_kernel(xf_ref, xb_ref, bf_ref, gf_ref, bb_ref, gb_ref,
                     of_ref, ob_ref, sf_ref, sb_ref, *, tb):
    n = pl.program_id(1)
    c = DN_CHUNK
    w = DN_WIDTH
    nch = tb // c

    @pl.when(n == 0)
    def _():
        sf_ref[...] = jnp.zeros_like(sf_ref)
        sb_ref[...] = jnp.zeros_like(sb_ref)

    r_i = lax.broadcasted_iota(jnp.int32, (w, w), 0)
    l_i = lax.broadcasted_iota(jnp.int32, (w, w), 1)
    bd_mask = (r_i // DN_HEAD_DIM) == (l_i // DN_HEAD_DIM)
    bd2_mask = jnp.concatenate([bd_mask, bd_mask], axis=1)

    def bd2(a, b):
        return jnp.concatenate([_block_diag(a, bd_mask), _block_diag(b, bd_mask)], axis=1)

    prep = (_dn_block_prep(False, xf_ref, bf_ref, gf_ref, tb),
            _dn_block_prep(True, xb_ref, bb_ref, gb_ref, tb))
    units = [(d, t if d == 0 else nch - 1 - t) for t in range(nch) for d in range(2)]

    st = []
    for d, ci in units:
        p = prep[d]
        rows = slice(ci * c, (ci + 1) * c)
        qc, kc, vc, bc, gc = p["q"][rows], p["k"][rows], p["v"][rows], p["beta"][rows], p["gc"][rows]
        gc_row = jnp.sum(jnp.where(p["row_tri"], p["g"][rows], 0.0), axis=0, keepdims=True)
        decay = jnp.where(p["tri"], jnp.exp(jnp.where(p["tri"], gc - gc_row, 0.0)), 0.0)
        g_end = gc[0:1] if d == 1 else gc[c - 1:c]
        eg = jnp.exp(gc)
        kb = kc.astype(BF16)
        gram = lax.dot_general(jnp.concatenate([qc.astype(BF16), kb], axis=0),
                               _block_diag(kb, bd_mask),
                               (((1,), (1,)), ((), ())), preferred_element_type=F32)
        lmat = jnp.where(p["strict"], gram[c:] * bc * decay, 0.0)
        st.append(dict(a_intra=(gram[:c] * decay).astype(BF16), lmat=lmat,
                       tinv=p["eye"] - jnp.where(p["lvl_masks"][0], lmat, 0.0),
                       rhs_u=(vc * bc).astype(BF16), rhs_w=(kc * bc * eg).astype(BF16),
                       q_dec=qc * eg, k_dec=(kc * jnp.exp(g_end - gc)).astype(BF16),
                       gamma=jnp.exp(g_end)))

    for lvl in range(1, int(math.log2(c))):
        m1 = []
        for (d, ci), u in zip(units, st):
            loff = jnp.where(prep[d]["lvl_masks"][lvl], u["lmat"], 0.0).astype(BF16)
            m1.append(jnp.dot(loff, _block_diag(u["tinv"].astype(BF16), bd_mask),
                              preferred_element_type=F32))
        for u, m in zip(st, m1):
            u["tinv"] = u["tinv"] - jnp.dot(u["tinv"].astype(BF16),
                                            _block_diag(m.astype(BF16), bd_mask),
                                            preferred_element_type=F32)

    for u in st:
        uw = jnp.dot(u["tinv"].astype(BF16), bd2(u["rhs_u"], u["rhs_w"]),
                     preferred_element_type=F32)
        u["uw"] = uw.astype(BF16)
    for u in st:
        uwb = u["uw"]
        kuw = lax.dot_general(u["k_dec"], uwb, (((0,), (0,)), ((), ())),
                              preferred_element_type=F32)
        kuw = jnp.where(bd2_mask, kuw, 0.0)
        auw = jnp.dot(u["a_intra"], bd2(uwb[:, :w], uwb[:, w:]),
                      preferred_element_type=F32)
        u["b_bd"] = kuw[:, :w]
        u["lhs"] = jnp.concatenate([kuw[:, w:].astype(BF16),
                                    (u["q_dec"] - auw[:, w:]).astype(BF16)], axis=0)
        u["e"] = auw[:, :w]

    s_refs = (sf_ref, sb_ref)
    o_refs = (of_ref, ob_ref)
    state = [sf_ref[...], sb_ref[...]]
    for (d, ci), u in zip(units, st):
        r = jnp.dot(u["lhs"], state[d].astype(BF16), preferred_element_type=F32)
        o_refs[d][0, ci * c:(ci + 1) * c, :] = r[w:] + u["e"]
        state[d] = state[d] * u["gamma"] - r[:w] + u["b_bd"]
    for d in range(2):
        s_refs[d][...] = state[d]


def _deltanet(qkv, beta, g, tb):
    b, s, _ = qkv.shape
    nb = s // tb
    w3 = 3 * DN_WIDTH
    fwd = lambda bi, n: (bi, n, 0)
    bwd = lambda bi, n: (bi, nb - 1 - n, 0)
    bwd1 = lambda bi, n: (bi, nb - 1 - n, 1)
    gate = lambda im: pl.BlockSpec((1, tb, DN_WIDTH), im)
    return pl.pallas_call(
        functools.partial(_deltanet_kernel, tb=tb),
        grid=(b, nb),
        in_specs=[
            pl.BlockSpec((1, tb, w3), fwd), pl.BlockSpec((1, tb, w3), bwd),
            gate(fwd), gate(fwd), gate(bwd1), gate(bwd1),
        ],
        out_specs=[pl.BlockSpec((1, tb, DN_WIDTH), fwd), pl.BlockSpec((1, tb, DN_WIDTH), bwd)],
        out_shape=[jax.ShapeDtypeStruct((b, s, DN_WIDTH), F32)] * 2,
        scratch_shapes=[pltpu.VMEM((DN_WIDTH, DN_WIDTH), F32)] * 2,
        compiler_params=pltpu.CompilerParams(
            dimension_semantics=("parallel", "arbitrary"), vmem_limit_bytes=VMEM_LIMIT),
        name="deltanet",
    )(qkv, qkv, beta, g, beta, g)


def _diffattn_kernel(q_ref, k_ref, vt_ref, lam_ref, nw_ref, o_ref, *, lam_init, tks):
    d = DIFF_HEAD_DIM
    dv = 2 * d
    q = q_ref[0]
    tq = q.shape[0]
    lane = lax.broadcasted_iota(jnp.int32, q.shape, 1)
    zero = jnp.zeros((), q.dtype)
    q_half = (jnp.where(lane < d, q, zero), jnp.where(lane >= d, q, zero))

    tk = k_ref.shape[1]
    ones = jnp.ones((ONES_ROWS, tks), BF16)

    def scores(sub, j):
        kb = k_ref[0, sub * tks:(sub + 1) * tks, :]
        return lax.dot_general(kb, q_half[j], (((1,), (1,)), ((), ())),
                               preferred_element_type=F32)

    chains = [(sub, j) for sub in range(tk // tks) for j in range(2)]
    m = [jnp.full((1, tq), -jnp.inf, F32)] * 2
    acc = [jnp.zeros((dv + ONES_ROWS, tq), F32)] * 2
    s_next = scores(*chains[0])
    for idx, (sub, j) in enumerate(chains):
        s = s_next
        if idx + 1 < len(chains):
            s_next = scores(*chains[idx + 1])
        vt = jnp.concatenate([vt_ref[0, 0, :, sub * tks:(sub + 1) * tks], ones], axis=0)
        m_new = jnp.maximum(m[j], jnp.max(s, axis=0, keepdims=True))
        alpha = jnp.exp2(m[j] - m_new)
        p = jnp.exp2(s - m_new).astype(BF16)
        acc[j] = alpha * acc[j] + jnp.dot(vt, p, preferred_element_type=F32)
        m[j] = m_new

    lv = lam_ref[...]
    lam = (jnp.exp(jnp.sum(lv[0:1] * lv[1:2])) - jnp.exp(jnp.sum(lv[2:3] * lv[3:4]))
           + lam_init)
    a0, a1 = acc
    o = a0[:dv] / a0[dv:dv + 1] - lam * (a1[:dv] / a1[dv:dv + 1])
    ms = jnp.mean(o * o, axis=0, keepdims=True)
    o = o * lax.rsqrt(ms + EPS) * nw_ref[...] * (1.0 - lam_init)
    o_ref[0] = o.T.astype(o_ref.dtype)


def _diffattn(dq, dk, dvt, lam_p, nw_col, lam_init, tq, tks):
    b, s, _ = dq.shape
    hw = 2 * DIFF_HEAD_DIM
    return pl.pallas_call(
        functools.partial(_diffattn_kernel, lam_init=lam_init, tks=tks),
        grid=(b, DIFF_HEADS, s // tq),
        in_specs=[
            pl.BlockSpec((1, tq, hw), lambda bi, h, qi: (bi, qi, h)),
            pl.BlockSpec((1, s, hw), lambda bi, h, qi: (bi, 0, h)),
            pl.BlockSpec((1, 1, hw, s), lambda bi, h, qi: (bi, h, 0, 0)),
            pl.BlockSpec((4, DIFF_HEAD_DIM), lambda bi, h, qi: (0, 0)),
            pl.BlockSpec((hw, 1), lambda bi, h, qi: (0, 0)),
        ],
        out_specs=pl.BlockSpec((1, tq, hw), lambda bi, h, qi: (bi, qi, h)),
        out_shape=jax.ShapeDtypeStruct((b, s, DIFF_WIDTH), BF16),
        compiler_params=pltpu.CompilerParams(
            dimension_semantics=("parallel", "parallel", "parallel"),
            vmem_limit_bytes=VMEM_LIMIT),
        name="diffattn",
    )(dq, dk, dvt, lam_p, nw_col)


def _swa_kernel(q_ref, kp_ref, km_ref, kn_ref, vp_ref, vm_ref, vn_ref, sink_ref, o_ref, *, tq):
    qi = pl.program_id(1)
    g = pl.program_id(2)
    nq = pl.num_programs(1)
    d = SWA_HEAD_DIM
    grp = SWA_HEADS // SWA_KV_HEADS
    q = q_ref[0]
    k = jnp.concatenate([kp_ref[0], km_ref[0], kn_ref[0]], axis=0)
    v = jnp.concatenate([vp_ref[0], vm_ref[0], vn_ref[0]], axis=0)
    nk = tq + 2 * WINDOW
    qpos = lax.broadcasted_iota(jnp.int32, (tq, nk), 0)
    koff = lax.broadcasted_iota(jnp.int32, (tq, nk), 1) - WINDOW
    valid = jnp.abs(koff - qpos) <= WINDOW
    valid &= (koff >= 0) | (qi > 0)
    valid &= (koff < tq) | (qi < nq - 1)
    lane = lax.broadcasted_iota(jnp.int32, q.shape, 1)
    zero = jnp.zeros((), q.dtype)
    out = jnp.zeros((tq, grp * d), F32)
    for j in range(grp):
        in_head = (lane >= j * d) & (lane < (j + 1) * d)
        s = lax.dot_general(jnp.where(in_head, q, zero), k, (((1,), (1,)), ((), ())),
                            preferred_element_type=F32)
        s = jnp.where(valid, s, -jnp.inf)
        sink = sink_ref[g * grp + j]
        m = jnp.maximum(jnp.max(s, axis=-1, keepdims=True), sink)
        p = jnp.exp(s - m)
        denom = jnp.sum(p, axis=-1, keepdims=True) + jnp.exp(sink - m)
        pv = jnp.dot(p.astype(BF16), v, preferred_element_type=F32)
        out = jnp.where(in_head, pv / denom, out)
    o_ref[0] = out.astype(o_ref.dtype)


def _swa(sq, sk, sv, sink, tq):
    b, s, _ = sq.shape
    nq = s // tq
    wpb = tq // WINDOW
    nw = s // WINDOW
    gw = (SWA_HEADS // SWA_KV_HEADS) * SWA_HEAD_DIM
    main = lambda bi, qi, g: (bi, qi, g)
    prev = lambda bi, qi, g: (bi, jnp.maximum(qi * wpb - 1, 0), g)
    nxt = lambda bi, qi, g: (bi, jnp.minimum((qi + 1) * wpb, nw - 1), g)
    return pl.pallas_call(
        functools.partial(_swa_kernel, tq=tq),
        grid=(b, nq, SWA_KV_HEADS),
        in_specs=[
            pl.BlockSpec((1, tq, gw), main),
            pl.BlockSpec((1, WINDOW, gw), prev), pl.BlockSpec((1, tq, gw), main),
            pl.BlockSpec((1, WINDOW, gw), nxt),
            pl.BlockSpec((1, WINDOW, gw), prev), pl.BlockSpec((1, tq, gw), main),
            pl.BlockSpec((1, WINDOW, gw), nxt),
            pl.BlockSpec(memory_space=pltpu.SMEM),
        ],
        out_specs=pl.BlockSpec((1, tq, gw), main),
        out_shape=jax.ShapeDtypeStruct((b, s, SWA_WIDTH), BF16),
        compiler_params=pltpu.CompilerParams(
            dimension_semantics=("parallel", "parallel", "parallel"),
            vmem_limit_bytes=VMEM_LIMIT),
        name="swa",
    )(sq, sk, sk, sk, sv, sv, sv, sink)


def _outmlp_kernel(x_ref, of_ref, ob_ref, z_ref, df_ref, sw_ref, dnw_ref, wo_ref,
                   n2_ref, wu_ref, wd_ref, fn_ref, y_ref, *, final_norm, ff_chunk, row_groups):
    w = DN_WIDTH
    r_i = lax.broadcasted_iota(jnp.int32, (w, w), 0)
    l_i = lax.broadcasted_iota(jnp.int32, (w, w), 1)
    ones_bd = ((r_i // DN_HEAD_DIM) == (l_i // DN_HEAD_DIM)).astype(BF16)
    tm = x_ref.shape[0]
    groups = [slice(r0, r0 + tm // row_groups) for r0 in range(0, tm, tm // row_groups)]

    def out_proj(rows):
        o = of_ref[rows, :] + ob_ref[rows, :]
        sq = o * o
        hi = sq.astype(BF16)
        lo = (sq - hi.astype(F32)).astype(BF16)
        ss = (jnp.dot(hi, ones_bd, preferred_element_type=F32)
              + jnp.dot(lo, ones_bd, preferred_element_type=F32))
        dn = o * lax.rsqrt(ss * (1.0 / DN_HEAD_DIM) + EPS) * dnw_ref[...]
        dn = dn * _silu(z_ref[rows, :].astype(F32))
        x = x_ref[rows, :]
        x = x + jnp.dot(dn.astype(BF16), wo_ref[0:w, :], preferred_element_type=F32)
        x = x + jnp.dot(df_ref[rows, :], wo_ref[w:w + DIFF_WIDTH, :], preferred_element_type=F32)
        return x + jnp.dot(sw_ref[rows, :], wo_ref[w + DIFF_WIDTH:, :], preferred_element_type=F32)

    def normed(x, nw_ref):
        ms = jnp.mean(x * x, axis=-1, keepdims=True)
        return x * lax.rsqrt(ms + EPS) * nw_ref[...]

    xs = [out_proj(rows) for rows in groups]
    hns = [normed(x, n2_ref).astype(BF16) for x in xs]
    mlps = [jnp.zeros_like(x) for x in xs]
    for c0 in range(0, D_FF, ff_chunk):
        for gi, hn in enumerate(hns):
            h = jnp.dot(hn, wu_ref[:, c0:c0 + ff_chunk], preferred_element_type=F32)
            h = jnp.square(jnp.maximum(h, 0.0)).astype(BF16)
            mlps[gi] = mlps[gi] + jnp.dot(h, wd_ref[c0:c0 + ff_chunk, :],
                                          preferred_element_type=F32)
    for rows, x, mlp in zip(groups, xs, mlps):
        x = x + mlp
        y_ref[rows, :] = normed(x, fn_ref) if final_norm else x


def _outmlp(x2d, o_f, o_b, z, df, sw, dnw_e, wo, n2, wu, wd, fn, final_norm, tm):
    t, d = x2d.shape
    row = lambda i: (i, 0)
    const = lambda i: (0, 0)
    resident = lambda shape: pl.BlockSpec(shape, const)
    return pl.pallas_call(
        functools.partial(_outmlp_kernel, final_norm=final_norm, ff_chunk=1024, row_groups=2),
        grid=(t // tm,),
        in_specs=[
            pl.BlockSpec((tm, d), row),
            pl.BlockSpec((tm, DN_WIDTH), row), pl.BlockSpec((tm, DN_WIDTH), row),
            pl.BlockSpec((tm, DN_WIDTH), row), pl.BlockSpec((tm, DIFF_WIDTH), row),
            pl.BlockSpec((tm, SWA_WIDTH), row),
            pl.BlockSpec((1, DN_WIDTH), const),
            resident((d, d)), pl.BlockSpec((1, d), const),
            resident((d, D_FF)), resident((D_FF, d)),
            pl.BlockSpec((1, d), const),
        ],
        out_specs=pl.BlockSpec((tm, d), row),
        out_shape=jax.ShapeDtypeStruct((t, d), F32),
        compiler_params=pltpu.CompilerParams(
            dimension_semantics=("parallel",), vmem_limit_bytes=VMEM_LIMIT),
        name="outmlp",
    )(x2d, o_f, o_b, z, df, sw, dnw_e, wo, n2, wu, wd, fn)


def _pick(n, pref):
    t = min(n, pref)
    assert n % t == 0, (n, pref)
    return t


def _trunk(x, params):
    b, s, d = x.shape
    assert d == D_MODEL and s % DN_CHUNK == 0 and s % WINDOW == 0
    depth = params["w_in_r"].shape[0]
    tm = _pick(s, 512)
    cos_t, sin_t = _rope_tables(s)
    x2d = x.reshape(b * s, d)
    for li in range(depth):
        p = {k: v[li] for k, v in params.items() if k != "final_norm_w"}
        lam_init = 0.8 - 0.6 * math.exp(-0.3 * li)
        qkv, z, beta, g, dq, dk, dv, sq, sk, sv = _inproj(
            x2d, s, p["norm1_w"], p["w_in_r"], cos_t, sin_t, p["alog_e"], p["dtb_e"],
            p["conv_w"], tm)
        r3 = lambda a: a.reshape(b, s, a.shape[-1])
        o_f, o_b = _deltanet(r3(qkv), r3(beta), r3(g), _pick(s, 512))
        df = _diffattn(r3(dq), r3(dk), dv, p["diff_lambda"], p["diff_norm_w"], lam_init,
                       _pick(s, 1024), _pick(s, 512))
        sw = _swa(r3(sq), r3(sk), r3(sv), p["swa_sink"], _pick(s, 512))
        r2 = lambda a: a.reshape(b * s, a.shape[-1])
        x2d = _outmlp(x2d, r2(o_f), r2(o_b), z, r2(df), r2(sw), p["dn_norm_e"], p["w_out"],
                      p["norm2_w"], p["w_up"], p["w_down"], params["final_norm_w"],
                      li == depth - 1, tm)
    return x2d.reshape(b, s, d)


def kernel(x_prompt, x_sample, norm1_w, w_in, dn_conv_w, dn_a_log, dn_dt_bias, dn_norm_w,
           diff_lambda, diff_norm_w, swa_sink, w_out, norm2_w, w_up, w_down, final_norm_w):
    depth = w_in.shape[0]
    rep = lambda a: jnp.repeat(a.astype(F32).reshape(depth, 1, 2 * DN_HEADS), DN_HEAD_DIM, axis=2)
    params = {
        "norm1_w": norm1_w.astype(F32)[:, None, :],
        "w_in_r": _reorder_w_in(w_in),
        "conv_w": jnp.pad(dn_conv_w.astype(F32), ((0, 0), (0, 8 - DN_CONV), (0, 0))),
        "alog_e": rep(dn_a_log),
        "dtb_e": rep(dn_dt_bias),
        "dn_norm_e": jnp.tile(dn_norm_w.astype(F32), (1, DN_HEADS))[:, None, :],
        "diff_lambda": diff_lambda.astype(F32),
        "diff_norm_w": diff_norm_w.astype(F32)[:, :, None],
        "swa_sink": swa_sink.astype(F32),
        "w_out": w_out.astype(BF16),
        "norm2_w": norm2_w.astype(F32)[:, None, :],
        "w_up": w_up.astype(BF16),
        "w_down": w_down.astype(BF16),
        "final_norm_w": final_norm_w.astype(F32)[None, :],
    }
    return _trunk(x_prompt, params), _trunk(x_sample, params)
```

```python
import functools
import math

import numpy as np
import jax
import jax.numpy as jnp
from jax import lax
from jax.experimental import pallas as pl
from jax.experimental.pallas import tpu as pltpu

F32 = jnp.float32
BF16 = jnp.bfloat16

D_MODEL = 1024
DN_HEADS = 4
DN_HEAD_DIM = 64
DN_WIDTH = DN_HEADS * DN_HEAD_DIM
DN_CONV = 5
DN_CHUNK = 64
DIFF_HEADS = 4
DIFF_HEAD_DIM = 64
DIFF_WIDTH = DIFF_HEADS * 2 * DIFF_HEAD_DIM
SWA_HEADS = 4
SWA_KV_HEADS = 2
SWA_HEAD_DIM = 64
SWA_WIDTH = SWA_HEADS * SWA_HEAD_DIM
WINDOW = 128
ROPE_THETA = 10000.0
D_FF = 4 * D_MODEL
EPS = 1e-6

LANES = 128
CONV_HALO = 8
ONES_ROWS = 16
LOG2E = math.log2(math.e)
VMEM_LIMIT = 56 * 1024 * 1024

_SEC_SIZES = (3 * DN_WIDTH, DN_WIDTH, 2 * DN_WIDTH, 2 * DN_WIDTH,
              DIFF_WIDTH, DIFF_WIDTH, DIFF_WIDTH, SWA_WIDTH, SWA_WIDTH, SWA_WIDTH)
_SEC_OFFS = tuple(int(v) for v in np.cumsum((0,) + _SEC_SIZES))
IN_COLS_R = _SEC_OFFS[-1]


def _reorder_w_in(w_in):
    depth, d, _ = w_in.shape
    w = w_in.astype(BF16)
    o_b = 4 * DN_WIDTH
    o_dq = o_b + 4 * DN_HEADS
    o_sk = o_dq + 3 * DIFF_WIDTH + SWA_WIDTH
    kv_w = SWA_KV_HEADS * SWA_HEAD_DIM
    grp = SWA_HEADS // SWA_KV_HEADS

    def per_group(cols):
        c = cols.reshape(depth, d, SWA_KV_HEADS, 1, SWA_HEAD_DIM)
        return jnp.broadcast_to(c, (depth, d, SWA_KV_HEADS, grp, SWA_HEAD_DIM)).reshape(depth, d, -1)

    out = jnp.concatenate([
        w[:, :, :o_b],
        jnp.repeat(w[:, :, o_b:o_dq], DN_HEAD_DIM, axis=2),
        w[:, :, o_dq:o_sk],
        per_group(w[:, :, o_sk:o_sk + kv_w]),
        per_group(w[:, :, o_sk + kv_w:o_sk + 2 * kv_w])], axis=2)
    assert out.shape[2] == IN_COLS_R
    return out


def _rope_tables(seq):
    half = DIFF_HEAD_DIM // 2
    inv = ROPE_THETA ** (-jnp.arange(half, dtype=F32) / half)
    ang = jnp.arange(seq, dtype=F32)[:, None] * inv[None, :]
    cos, sin = jnp.cos(ang), jnp.sin(ang)
    reps = LANES // (2 * half)
    cos_t = jnp.tile(jnp.concatenate([cos, cos], axis=1), (1, reps))
    sin_t = jnp.tile(jnp.concatenate([-sin, sin], axis=1), (1, reps))
    return cos_t, sin_t


def _sigmoid(x):
    return 1.0 / (1.0 + jnp.exp(-x))


def _silu(x):
    return x * _sigmoid(x)


def _inproj_kernel(x_ref, xp_ref, xn_ref, nw_ref, w_ref, cos_ref, sin_ref, alog_ref, dtb_ref,
                   cw_ref, qkv_ref, z_ref, beta_ref, g_ref, dq_ref, dk_ref, dv_ref,
                   sq_ref, sk_ref, sv_ref, *, nseq):
    def normed(x):
        ms = jnp.mean(x * x, axis=-1, keepdims=True)
        return (x * lax.rsqrt(ms + EPS) * nw_ref[...]).astype(BF16)

    hn = normed(x_ref[...])
    cos = cos_ref[...]
    sin = sin_ref[...]
    lane = lax.broadcasted_iota(jnp.int32, cos.shape, 1)
    low = (lane % DIFF_HEAD_DIM) < (DIFF_HEAD_DIM // 2)

    def proj(sec):
        return jnp.dot(hn, w_ref[:, _SEC_OFFS[sec]:_SEC_OFFS[sec + 1]],
                       preferred_element_type=F32)

    def rope_store(sec, out_ref, scale):
        y_all = proj(sec)
        for c0 in range(0, _SEC_SIZES[sec], LANES):
            y = y_all[:, c0:c0 + LANES]
            up = pltpu.roll(y, DIFF_HEAD_DIM // 2, axis=1)
            down = pltpu.roll(y, LANES - DIFF_HEAD_DIM // 2, axis=1)
            r = y * cos + jnp.where(low, down, up) * sin
            if scale != 1.0:
                r = r * scale
            out_ref[:, c0:c0 + LANES] = r.astype(out_ref.dtype)

    tm = hn.shape[0]
    w = DN_WIDTH
    tile = pl.program_id(0) % nseq
    halo = jnp.dot(normed(jnp.concatenate([xp_ref[...], xn_ref[...]], axis=0)),
                   w_ref[:, 0:3 * w], preferred_element_type=F32)
    ext = jnp.concatenate([jnp.where(tile > 0, halo[:CONV_HALO], 0.0), proj(0),
                           jnp.where(tile < nseq - 1, halo[CONV_HALO:], 0.0)], axis=0)
    z_ref[...] = proj(1).astype(z_ref.dtype)
    beta_ref[...] = _sigmoid(proj(2))
    a = proj(3) + dtb_ref[...]
    softplus = jnp.maximum(a, 0.0) + jnp.log(1.0 + jnp.exp(-jnp.abs(a)))
    g_ref[...] = -jnp.exp(alog_ref[...]) * softplus
    rope_store(4, dq_ref, DIFF_HEAD_DIM ** -0.5 * LOG2E)
    rope_store(5, dk_ref, 1.0)
    dv = proj(6)
    hw = 2 * DIFF_HEAD_DIM
    for h in range(DIFF_HEADS):
        dv_ref[0, h] = dv[:, h * hw:(h + 1) * hw].T.astype(dv_ref.dtype)
    rope_store(7, sq_ref, SWA_HEAD_DIM ** -0.5)
    rope_store(8, sk_ref, 1.0)
    sv_ref[...] = proj(9).astype(sv_ref.dtype)

    n_ext = tm + 2 * CONV_HALO
    cw = cw_ref[...]
    y = jnp.zeros((tm, 3 * w), F32)
    for tap in range(DN_CONV):
        off = tap - DN_CONV // 2
        sh = ext if off == 0 else pltpu.roll(ext, (-off) % n_ext, axis=0)
        y = y + sh[CONV_HALO:CONV_HALO + tm] * cw[tap:tap + 1, :]
    y = _silu(y)
    r_i = lax.broadcasted_iota(jnp.int32, (w, w), 0)
    l_i = lax.broadcasted_iota(jnp.int32, (w, w), 1)
    ones_bd = ((r_i // DN_HEAD_DIM) == (l_i // DN_HEAD_DIM)).astype(BF16)

    def seg_sum(t):
        hi = t.astype(BF16)
        lo = (t - hi.astype(F32)).astype(BF16)
        return (jnp.dot(hi, ones_bd, preferred_element_type=F32)
                + jnp.dot(lo, ones_bd, preferred_element_type=F32))

    q, k = y[:, :w], y[:, w:2 * w]
    q = q * lax.rsqrt(seg_sum(q * q) + EPS) * (DN_HEAD_DIM ** -0.5)
    k = k * lax.rsqrt(seg_sum(k * k) + EPS)
    qkv_ref[:, 0:w] = q.astype(qkv_ref.dtype)
    qkv_ref[:, w:2 * w] = k.astype(qkv_ref.dtype)
    qkv_ref[:, 2 * w:] = y[:, 2 * w:].astype(qkv_ref.dtype)


def _inproj(x2d, seq, nw, w_r, cos_t, sin_t, alog_e, dtb_e, cw, tm):
    t, d = x2d.shape
    nseq = seq // tm
    row = lambda i: (i, 0)
    const = lambda i: (0, 0)
    widths = (3 * DN_WIDTH, DN_WIDTH, 2 * DN_WIDTH, 2 * DN_WIDTH, DIFF_WIDTH, DIFF_WIDTH,
              DIFF_WIDTH, SWA_WIDTH, SWA_WIDTH, SWA_WIDTH)
    dtypes = (BF16, BF16, F32, F32, BF16, BF16, BF16, BF16, BF16, BF16)
    dvt_pos = 6
    hw = 2 * DIFF_HEAD_DIM
    dvt_spec = pl.BlockSpec((1, DIFF_HEADS, hw, tm), lambda i: (i // nseq, 0, 0, i % nseq))
    dvt_shape = jax.ShapeDtypeStruct((t // seq, DIFF_HEADS, hw, seq), BF16)
    hpt = tm // CONV_HALO
    last = t // CONV_HALO - 1
    return pl.pallas_call(
        functools.partial(_inproj_kernel, nseq=nseq),
        grid=(t // tm,),
        in_specs=[
            pl.BlockSpec((tm, d), row),
            pl.BlockSpec((CONV_HALO, d), lambda i: (jnp.maximum(i * hpt - 1, 0), 0)),
            pl.BlockSpec((CONV_HALO, d), lambda i: (jnp.minimum((i + 1) * hpt, last), 0)),
            pl.BlockSpec((1, d), const),
            pl.BlockSpec((d, IN_COLS_R), const, pipeline_mode=pl.Buffered(1)),
            pl.BlockSpec((tm, LANES), lambda i: (i % nseq, 0)),
            pl.BlockSpec((tm, LANES), lambda i: (i % nseq, 0)),
            pl.BlockSpec((1, 2 * DN_WIDTH), const),
            pl.BlockSpec((1, 2 * DN_WIDTH), const),
            pl.BlockSpec((8, 3 * DN_WIDTH), const),
        ],
        out_specs=[dvt_spec if k == dvt_pos else pl.BlockSpec((tm, w), row)
                   for k, w in enumerate(widths)],
        out_shape=[dvt_shape if k == dvt_pos else jax.ShapeDtypeStruct((t, w), dt)
                   for k, (w, dt) in enumerate(zip(widths, dtypes))],
        compiler_params=pltpu.CompilerParams(
            dimension_semantics=("parallel",), vmem_limit_bytes=VMEM_LIMIT),
        name="inproj",
    )(x2d, x2d, x2d, nw, w_r, cos_t, sin_t, alog_e, dtb_e, cw)


def _block_diag(m, bd_mask):
    return jnp.where(bd_mask, jnp.concatenate([m] * DN_HEADS, axis=0), jnp.zeros((), m.dtype))


def _dn_block_prep(reverse, x_ref, b_ref, g_ref, tb):
    c = DN_CHUNK
    w = DN_WIDTH
    y = x_ref[0].astype(F32)
    q, k, v = y[:, :w], y[:, w:2 * w], y[:, 2 * w:]
    g = g_ref[0]

    i_c = lax.broadcasted_iota(jnp.int32, (c, w), 0)
    j_c = lax.broadcasted_iota(jnp.int32, (c, w), 1) % DN_HEAD_DIM
    if reverse:
        tri, strict, row_tri = i_c <= j_c, i_c < j_c, i_c >= j_c
    else:
        tri, strict, row_tri = i_c >= j_c, i_c > j_c, i_c <= j_c
    lvl_masks = []
    for lvl in range(int(math.log2(c))):
        siblings = (i_c >> (lvl + 1)) == (j_c >> (lvl + 1))
        odd_i = ((i_c >> lvl) & 1) == 1
        odd_j = ((j_c >> lvl) & 1) == 1
        lvl_masks.append(siblings & ((~odd_i & odd_j) if reverse else (odd_i & ~odd_j)))

    r_b = lax.broadcasted_iota(jnp.int32, (c, c), 0)
    c_b = lax.broadcasted_iota(jnp.int32, (c, c), 1)
    cum = ((r_b <= c_b) if reverse else (r_b >= c_b)).astype(BF16)
    g_hi = g.astype(BF16)
    g_r1 = g - g_hi.astype(F32)
    g_mid = g_r1.astype(BF16)
    g_lo = (g_r1 - g_mid.astype(F32)).astype(BF16)
    g3 = jnp.concatenate([g_hi, g_mid, g_lo], axis=1)
    gc_chunks = []
    for ci in range(tb // c):
        t3 = jnp.dot(cum, g3[ci * c:(ci + 1) * c], preferred_element_type=F32)
        gc_chunks.append(t3[:, :w] + t3[:, w:2 * w] + t3[:, 2 * w:])
    gc_all = jnp.concatenate(gc_chunks, axis=0)
    return dict(q=q, k=k, v=v, beta=b_ref[0], g=g, gc=gc_all, tri=tri, strict=strict,
                row_tri=row_tri, eye=(i_c == j_c).astype(F32), lvl_masks=lvl_masks)


def _deltanet_kernel(xf_ref, xb_ref, bf_ref, gf_ref, bb_ref, gb_ref,
                     of_ref, ob_ref, sf_ref, sb_ref, *, tb):
    n = pl.program_id(1)
    c = DN_CHUNK
    w = DN_WIDTH
    nch = tb // c

    @pl.when(n == 0)
    def _():
        sf_ref[...] = jnp.zeros_like(sf_ref)
        sb_ref[...] = jnp.zeros_like(sb_ref)

    r_i = lax.broadcasted_iota(jnp.int32, (w, w), 0)
    l_i = lax.broadcasted_iota(jnp.int32, (w, w), 1)
    bd_mask = (r_i // DN_HEAD_DIM) == (l_i // DN_HEAD_DIM)
    bd2_mask = jnp.concatenate([bd_mask, bd_mask], axis=1)

    def bd2(a, b):
        return jnp.concatenate([_block_diag(a, bd_mask), _block_diag(b, bd_mask)], axis=1)

    prep = (_dn_block_prep(False, xf_ref, bf_ref, gf_ref, tb),
            _dn_block_prep(True, xb_ref, bb_ref, gb_ref, tb))
    units = [(d, t if d == 0 else nch - 1 - t) for t in range(nch) for d in range(2)]

    st = []
    for d, ci in units:
        p = prep[d]
        rows = slice(ci * c, (ci + 1) * c)
        qc, kc, vc, bc, gc = p["q"][rows], p["k"][rows], p["v"][rows], p["beta"][rows], p["gc"][rows]
        gc_row = jnp.sum(jnp.where(p["row_tri"], p["g"][rows], 0.0), axis=0, keepdims=True)
        decay = jnp.where(p["tri"], jnp.exp(jnp.where(p["tri"], gc - gc_row, 0.0)), 0.0)
        g_end = gc[0:1] if d == 1 else gc[c - 1:c]
        eg = jnp.exp(gc)
        kb = kc.astype(BF16)
        gram = lax.dot_general(jnp.concatenate([qc.astype(BF16), kb], axis=0),
                               _block_diag(kb, bd_mask),
                               (((1,), (1,)), ((), ())), preferred_element_type=F32)
        lmat = jnp.where(p["strict"], gram[c:] * bc * decay, 0.0)
        st.append(dict(a_intra=(gram[:c] * decay).astype(BF16), lmat=lmat,
                       tinv=p["eye"] - jnp.where(p["lvl_masks"][0], lmat, 0.0),
                       rhs_u=(vc * bc).astype(BF16), rhs_w=(kc * bc * eg).astype(BF16),
                       q_dec=qc * eg, k_dec=(kc * jnp.exp(g_end - gc)).astype(BF16),
                       gamma=jnp.exp(g_end)))

    for lvl in range(1, int(math.log2(c))):
        m1 = []
        for (d, ci), u in zip(units, st):
            loff = jnp.where(prep[d]["lvl_masks"][lvl], u["lmat"], 0.0).astype(BF16)
            m1.append(jnp.dot(loff, _block_diag(u["tinv"].astype(BF16), bd_mask),
                              preferred_element_type=F32))
        for u, m in zip(st, m1):
            u["tinv"] = u["tinv"] - jnp.dot(u["tinv"].astype(BF16),
                                            _block_diag(m.astype(BF16), bd_mask),
                                            preferred_element_type=F32)

    for u in st:
        uw = jnp.dot(u["tinv"].astype(BF16), bd2(u["rhs_u"], u["rhs_w"]),
                     preferred_element_type=F32)
        u["uw"] = uw.astype(BF16)
    for u in st:
        uwb = u["uw"]
        kuw = lax.dot_general(u["k_dec"], uwb, (((0,), (0,)), ((), ())),
                              preferred_element_type=F32)
        kuw = jnp.where(bd2_mask, kuw, 0.0)
        auw = jnp.dot(u["a_intra"], bd2(uwb[:, :w], uwb[:, w:]),
                      preferred_element_type=F32)
        u["b_bd"] = kuw[:, :w]
        u["lhs"] = jnp.concatenate([kuw[:, w:].astype(BF16),
                                    (u["q_dec"] - auw[:, w:]).astype(BF16)], axis=0)
        u["e"] = auw[:, :w]

    s_refs = (sf_ref, sb_ref)
    o_refs = (of_ref, ob_ref)
    state = [sf_ref[...], sb_ref[...]]
    for (d, ci), u in zip(units, st):
        r = jnp.dot(u["lhs"], state[d].astype(BF16), preferred_element_type=F32)
        o_refs[d][0, ci * c:(ci + 1) * c, :] = r[w:] + u["e"]
        state[d] = state[d] * u["gamma"] - r[:w] + u["b_bd"]
    for d in range(2):
        s_refs[d][...] = state[d]


def _deltanet(qkv, beta, g, tb):
    b, s, _ = qkv.shape
    nb = s // tb
    w3 = 3 * DN_WIDTH
    fwd = lambda bi, n: (bi, n, 0)
    bwd = lambda bi, n: (bi, nb - 1 - n, 0)
    bwd1 = lambda bi, n: (bi, nb - 1 - n, 1)
    gate = lambda im: pl.BlockSpec((1, tb, DN_WIDTH), im)
    return pl.pallas_call(
        functools.partial(_deltanet_kernel, tb=tb),
        grid=(b, nb),
        in_specs=[
            pl.BlockSpec((1, tb, w3), fwd), pl.BlockSpec((1, tb, w3), bwd),
            gate(fwd), gate(fwd), gate(bwd1), gate(bwd1),
        ],
        out_specs=[pl.BlockSpec((1, tb, DN_WIDTH), fwd), pl.BlockSpec((1, tb, DN_WIDTH), bwd)],
        out_shape=[jax.ShapeDtypeStruct((b, s, DN_WIDTH), F32)] * 2,
        scratch_shapes=[pltpu.VMEM((DN_WIDTH, DN_WIDTH), F32)] * 2,
        compiler_params=pltpu.CompilerParams(
            dimension_semantics=("parallel", "arbitrary"), vmem_limit_bytes=VMEM_LIMIT),
        name="deltanet",
    )(qkv, qkv, beta, g, beta, g)


def _diffattn_kernel(q_ref, k_ref, vt_ref, lam_ref, nw_ref, o_ref, *, lam_init, tks):
    d = DIFF_HEAD_DIM
    dv = 2 * d
    q = q_ref[0]
    tq = q.shape[0]
    lane = lax.broadcasted_iota(jnp.int32, q.shape, 1)
    zero = jnp.zeros((), q.dtype)
    q_half = (jnp.where(lane < d, q, zero), jnp.where(lane >= d, q, zero))

    tk = k_ref.shape[1]
    ones = jnp.ones((ONES_ROWS, tks), BF16)

    def scores(sub, j):
        kb = k_ref[0, sub * tks:(sub + 1) * tks, :]
        return lax.dot_general(kb, q_half[j], (((1,), (1,)), ((), ())),
                               preferred_element_type=F32)

    chains = [(sub, j) for sub in range(tk // tks) for j in range(2)]
    m = [jnp.full((1, tq), -jnp.inf, F32)] * 2
    acc = [jnp.zeros((dv + ONES_ROWS, tq), F32)] * 2
    s_next = scores(*chains[0])
    for idx, (sub, j) in enumerate(chains):
        s = s_next
        if idx + 1 < len(chains):
            s_next = scores(*chains[idx + 1])
        vt = jnp.concatenate([vt_ref[0, 0, :, sub * tks:(sub + 1) * tks], ones], axis=0)
        m_new = jnp.maximum(m[j], jnp.max(s, axis=0, keepdims=True))
        alpha = jnp.exp2(m[j] - m_new)
        p = jnp.exp2(s - m_new).astype(BF16)
        acc[j] = alpha * acc[j] + jnp.dot(vt, p, preferred_element_type=F32)
        m[j] = m_new

    lv = lam_ref[...]
    lam = (jnp.exp(jnp.sum(lv[0:1] * lv[1:2])) - jnp.exp(jnp.sum(lv[2:3] * lv[3:4]))
           + lam_init)
    a0, a1 = acc
    o = a0[:dv] / a0[dv:dv + 1] - lam * (a1[:dv] / a1[dv:dv + 1])
    ms = jnp.mean(o * o, axis=0, keepdims=True)
    o = o * lax.rsqrt(ms + EPS) * nw_ref[...] * (1.0 - lam_init)
    o_ref[0] = o.T.astype(o_ref.dtype)


def _diffattn(dq, dk, dvt, lam_p, nw_col, lam_init, tq, tks):
    b, s, _ = dq.shape
    hw = 2 * DIFF_HEAD_DIM
    return pl.pallas_call(
        functools.partial(_diffattn_kernel, lam_init=lam_init, tks=tks),
        grid=(b, DIFF_HEADS, s // tq),
        in_specs=[
            pl.BlockSpec((1, tq, hw), lambda bi, h, qi: (bi, qi, h)),
            pl.BlockSpec((1, s, hw), lambda bi, h, qi: (bi, 0, h)),
            pl.BlockSpec((1, 1, hw, s), lambda bi, h, qi: (bi, h, 0, 0)),
            pl.BlockSpec((4, DIFF_HEAD_DIM), lambda bi, h, qi: (0, 0)),
            pl.BlockSpec((hw, 1), lambda bi, h, qi: (0, 0)),
        ],
        out_specs=pl.BlockSpec((1, tq, hw), lambda bi, h, qi: (bi, qi, h)),
        out_shape=jax.ShapeDtypeStruct((b, s, DIFF_WIDTH), BF16),
        compiler_params=pltpu.CompilerParams(
            dimension_semantics=("parallel", "parallel", "parallel"),
            vmem_limit_bytes=VMEM_LIMIT),
        name="diffattn",
    )(dq, dk, dvt, lam_p, nw_col)


def _swa_kernel(q_ref, kp_ref, km_ref, kn_ref, vp_ref, vm_ref, vn_ref, sink_ref, o_ref, *,
                tq, qb):
    qi = pl.program_id(1)
    g = pl.program_id(2)
    nq = pl.num_programs(1)
    d = SWA_HEAD_DIM
    grp = SWA_HEADS // SWA_KV_HEADS
    k = jnp.concatenate([kp_ref[0], km_ref[0], kn_ref[0]], axis=0)
    v = jnp.concatenate([vp_ref[0], vm_ref[0], vn_ref[0]], axis=0)
    nk = qb + 2 * WINDOW
    nblk = tq // qb
    qpos = lax.broadcasted_iota(jnp.int32, (qb, nk), 0)
    koff = lax.broadcasted_iota(jnp.int32, (qb, nk), 1) - WINDOW
    band = jnp.abs(koff - qpos) <= WINDOW
    lane = lax.broadcasted_iota(jnp.int32, (qb, grp * d), 1)
    zero = jnp.zeros((), q_ref.dtype)
    for bi in range(nblk):
        q = q_ref[0, bi * qb:(bi + 1) * qb, :]
        kb = k[bi * qb:bi * qb + nk]
        vb = v[bi * qb:bi * qb + nk]
        valid = band
        if bi == 0:
            valid = valid & ((koff >= 0) | (qi > 0))
        if bi == nblk - 1:
            valid = valid & ((koff < qb) | (qi < nq - 1))
        out = jnp.zeros((qb, grp * d), F32)
        for j in range(grp):
            in_head = (lane >= j * d) & (lane < (j + 1) * d)
            s = lax.dot_general(jnp.where(in_head, q, zero), kb, (((1,), (1,)), ((), ())),
                                preferred_element_type=F32)
            s = jnp.where(valid, s, -jnp.inf)
            sink = sink_ref[g * grp + j]
            m = jnp.maximum(jnp.max(s, axis=-1, keepdims=True), sink)
            p = jnp.exp(s - m)
            denom = jnp.sum(p, axis=-1, keepdims=True) + jnp.exp(sink - m)
            pv = jnp.dot(p.astype(BF16), vb, preferred_element_type=F32)
            out = jnp.where(in_head, pv / denom, out)
        o_ref[0, bi * qb:(bi + 1) * qb, :] = out.astype(o_ref.dtype)


def _swa(sq, sk, sv, sink, tq):
    b, s, _ = sq.shape
    nq = s // tq
    wpb = tq // WINDOW
    nw = s // WINDOW
    gw = (SWA_HEADS // SWA_KV_HEADS) * SWA_HEAD_DIM
    main = lambda bi, qi, g: (bi, qi, g)
    prev = lambda bi, qi, g: (bi, jnp.maximum(qi * wpb - 1, 0), g)
    nxt = lambda bi, qi, g: (bi, jnp.minimum((qi + 1) * wpb, nw - 1), g)
    return pl.pallas_call(
        functools.partial(_swa_kernel, tq=tq, qb=min(tq, 2 * WINDOW)),
        grid=(b, nq, SWA_KV_HEADS),
        in_specs=[
            pl.BlockSpec((1, tq, gw), main),
            pl.BlockSpec((1, WINDOW, gw), prev), pl.BlockSpec((1, tq, gw), main),
            pl.BlockSpec((1, WINDOW, gw), nxt),
            pl.BlockSpec((1, WINDOW, gw), prev), pl.BlockSpec((1, tq, gw), main),
            pl.BlockSpec((1, WINDOW, gw), nxt),
            pl.BlockSpec(memory_space=pltpu.SMEM),
        ],
        out_specs=pl.BlockSpec((1, tq, gw), main),
        out_shape=jax.ShapeDtypeStruct((b, s, SWA_WIDTH), BF16),
        compiler_params=pltpu.CompilerParams(
            dimension_semantics=("parallel", "parallel", "parallel"),
            vmem_limit_bytes=VMEM_LIMIT),
        name="swa",
    )(sq, sk, sk, sk, sv, sv, sv, sink)


def _outmlp_kernel(x_ref, of_ref, ob_ref, z_ref, df_ref, sw_ref, dnw_ref, wo_ref,
                   n2_ref, wu_ref, wd_ref, fn_ref, y_ref, *, final_norm, ff_chunk, row_groups):
    w = DN_WIDTH
    r_i = lax.broadcasted_iota(jnp.int32, (w, w), 0)
    l_i = lax.broadcasted_iota(jnp.int32, (w, w), 1)
    ones_bd = ((r_i // DN_HEAD_DIM) == (l_i // DN_HEAD_DIM)).astype(BF16)
    tm = x_ref.shape[0]
    groups = [slice(r0, r0 + tm // row_groups) for r0 in range(0, tm, tm // row_groups)]

    def out_proj(rows):
        o = of_ref[rows, :] + ob_ref[rows, :]
        sq = o * o
        hi = sq.astype(BF16)
        lo = (sq - hi.astype(F32)).astype(BF16)
        ss = (jnp.dot(hi, ones_bd, preferred_element_type=F32)
              + jnp.dot(lo, ones_bd, preferred_element_type=F32))
        dn = o * lax.rsqrt(ss * (1.0 / DN_HEAD_DIM) + EPS) * dnw_ref[...]
        dn = dn * _silu(z_ref[rows, :].astype(F32))
        x = x_ref[rows, :]
        x = x + jnp.dot(dn.astype(BF16), wo_ref[0:w, :], preferred_element_type=F32)
        x = x + jnp.dot(df_ref[rows, :], wo_ref[w:w + DIFF_WIDTH, :], preferred_element_type=F32)
        return x + jnp.dot(sw_ref[rows, :], wo_ref[w + DIFF_WIDTH:, :], preferred_element_type=F32)

    def normed(x, nw_ref):
        ms = jnp.mean(x * x, axis=-1, keepdims=True)
        return x * lax.rsqrt(ms + EPS) * nw_ref[...]

    xs = [out_proj(rows) for rows in groups]
    hns = [normed(x, n2_ref).astype(BF16) for x in xs]
    mlps = [jnp.zeros_like(x) for x in xs]
    for c0 in range(0, D_FF, ff_chunk):
        for gi, hn in enumerate(hns):
            h = jnp.dot(hn, wu_ref[:, c0:c0 + ff_chunk], preferred_element_type=F32)
            h = jnp.square(jnp.maximum(h, 0.0)).astype(BF16)
            mlps[gi] = mlps[gi] + jnp.dot(h, wd_ref[c0:c0 + ff_chunk, :],
                                          preferred_element_type=F32)
    for rows, x, mlp in zip(groups, xs, mlps):
        x = x + mlp
        y_ref[rows, :] = normed(x, fn_ref) if final_norm else x


def _outmlp(x2d, o_f, o_b, z, df, sw, dnw_e, wo, n2, wu, wd, fn, final_norm, tm):
    t, d = x2d.shape
    row = lambda i: (i, 0)
    const = lambda i: (0, 0)
    resident = lambda shape: pl.BlockSpec(shape, const)
    return pl.pallas_call(
        functools.partial(_outmlp_kernel, final_norm=final_norm, ff_chunk=1024, row_groups=2),
        grid=(t // tm,),
        in_specs=[
            pl.BlockSpec((tm, d), row),
            pl.BlockSpec((tm, DN_WIDTH), row), pl.BlockSpec((tm, DN_WIDTH), row),
            pl.BlockSpec((tm, DN_WIDTH), row), pl.BlockSpec((tm, DIFF_WIDTH), row),
            pl.BlockSpec((tm, SWA_WIDTH), row),
            pl.BlockSpec((1, DN_WIDTH), const),
            resident((d, d)), pl.BlockSpec((1, d), const),
            resident((d, D_FF)), resident((D_FF, d)),
            pl.BlockSpec((1, d), const),
        ],
        out_specs=pl.BlockSpec((tm, d), row),
        out_shape=jax.ShapeDtypeStruct((t, d), F32),
        compiler_params=pltpu.CompilerParams(
            dimension_semantics=("parallel",), vmem_limit_bytes=VMEM_LIMIT),
        name="outmlp",
    )(x2d, o_f, o_b, z, df, sw, dnw_e, wo, n2, wu, wd, fn)


def _pick(n, pref):
    t = min(n, pref)
    assert n % t == 0, (n, pref)
    return t


def _trunk(x, params):
    b, s, d = x.shape
    assert d == D_MODEL and s % DN_CHUNK == 0 and s % WINDOW == 0
    depth = params["w_in_r"].shape[0]
    tm = _pick(s, 512)
    cos_t, sin_t = _rope_tables(s)
    x2d = x.reshape(b * s, d)
    for li in range(depth):
        p = {k: v[li] for k, v in params.items() if k != "final_norm_w"}
        lam_init = 0.8 - 0.6 * math.exp(-0.3 * li)
        qkv, z, beta, g, dq, dk, dv, sq, sk, sv = _inproj(
            x2d, s, p["norm1_w"], p["w_in_r"], cos_t, sin_t, p["alog_e"], p["dtb_e"],
            p["conv_w"], tm)
        r3 = lambda a: a.reshape(b, s, a.shape[-1])
        o_f, o_b = _deltanet(r3(qkv), r3(beta), r3(g), _pick(s, 512))
        df = _diffattn(r3(dq), r3(dk), dv, p["diff_lambda"], p["diff_norm_w"], lam_init,
                       _pick(s, 1024), _pick(s, 512))
        sw = _swa(r3(sq), r3(sk), r3(sv), p["swa_sink"], _pick(s, 512))
        r2 = lambda a: a.reshape(b * s, a.shape[-1])
        x2d = _outmlp(x2d, r2(o_f), r2(o_b), z, r2(df), r2(sw), p["dn_norm_e"], p["w_out"],
                      p["norm2_w"], p["w_up"], p["w_down"], params["final_norm_w"],
                      li == depth - 1, tm)
    return x2d.reshape(b, s, d)


def kernel(x_prompt, x_sample, norm1_w, w_in, dn_conv_w, dn_a_log, dn_dt_bias, dn_norm_w,
           diff_lambda, diff_norm_w, swa_sink, w_out, norm2_w, w_up, w_down, final_norm_w):
    depth = w_in.shape[0]
    rep = lambda a: jnp.repeat(a.astype(F32).reshape(depth, 1, 2 * DN_HEADS), DN_HEAD_DIM, axis=2)
    params = {
        "norm1_w": norm1_w.astype(F32)[:, None, :],
        "w_in_r": _reorder_w_in(w_in),
        "conv_w": jnp.pad(dn_conv_w.astype(F32), ((0, 0), (0, 8 - DN_CONV), (0, 0))),
        "alog_e": rep(dn_a_log),
        "dtb_e": rep(dn_dt_bias),
        "dn_norm_e": jnp.tile(dn_norm_w.astype(F32), (1, DN_HEADS))[:, None, :],
        "diff_lambda": diff_lambda.astype(F32),
        "diff_norm_w": diff_norm_w.astype(F32)[:, :, None],
        "swa_sink": swa_sink.astype(F32),
        "w_out": w_out.astype(BF16),
        "norm2_w": norm2_w.astype(F32)[:, None, :],
        "w_up": w_up.astype(BF16),
        "w_down": w_down.astype(BF16),
        "final_norm_w": final_norm_w.astype(F32)[None, :],
    }
    return _trunk(x_prompt, params), _trunk(x_sample, params)
```
